```python
import math
import jax
import jax.numpy as jnp
from jax import lax
import numpy as np

D_MODEL = 1024
BATCH = 4
SEQ = 4096
DEPTH = 4

CHUNK = 64
N_META = 16
N_MIXERS = 4
EPS = 1e-6
NEG_INF = -1e30
ROPE_BASE = 10000.0
Q_BLOCK = 128

MLA_HEADS = 8
MLA_NOPE = 128
MLA_ROPE = 64
MLA_V = 128
MLA_QK = MLA_NOPE + MLA_ROPE
MLA_Q_LORA = 384
MLA_KV_LORA = 256

HGRN_EXPAND = 128
HGRN_HEADS = D_MODEL // HGRN_EXPAND
HGRN_DK = HGRN_EXPAND
HGRN_DV = D_MODEL // HGRN_HEADS
HGRN_CHUNK = 16

S5_GROUP = 16
S5_GROUPS = D_MODEL // S5_GROUP
S5_STATE = 64
S5_DT_MIN = 1e-3
S5_DT_MAX = 1e-1

RET_HEADS = 4
RET_DK = D_MODEL // RET_HEADS
RET_DV = 2 * RET_DK

FFN_HIDDEN = 2816
CONV_WIDTH = 3

N_MLA_L = (DEPTH + 3) // 4
N_HGRN_L = (DEPTH + 2) // 4
N_S5_L = (DEPTH + 1) // 4
N_RET_L = DEPTH // 4

kernel_name = "chunk_causal_hybrid_trunk"

F32 = jnp.float32


def rms_norm(x, g):
    xf = x.astype(F32)
    y = xf * lax.rsqrt(jnp.mean(xf * xf, axis=-1, keepdims=True) + EPS)
    return (y * g.astype(F32)).astype(x.dtype)


def rope_tables(n_pos, dim):
    inv_freq = 1.0 / (ROPE_BASE ** (jnp.arange(0, dim, 2, dtype=F32) / dim))
    ang = jnp.arange(n_pos, dtype=F32)[:, None] * inv_freq[None, :]
    return jnp.cos(ang), jnp.sin(ang)


def apply_rope(x, cos, sin):
    half = x.shape[-1] // 2
    x1, x2 = x[..., :half], x[..., half:]
    c = cos[None, :, None, :].astype(x.dtype)
    s = sin[None, :, None, :].astype(x.dtype)
    return jnp.concatenate([x1 * c - x2 * s, x1 * s + x2 * c], axis=-1)


def pad_front(t, n):
    return jnp.pad(t, [(0, 0), (n, 0)] + [(0, 0)] * (t.ndim - 2))


def chunk_ids(n_pos):
    real = 1 + jnp.arange(n_pos - N_META, dtype=jnp.int32) // CHUNK
    return jnp.concatenate([jnp.zeros((N_META,), jnp.int32), real])


def mla_mixer(a, w_down, g_cq, g_ckv, w_uq, w_ukv, g_qhead, g_khead, w_o, cos, sin, cid):
    B, L, _ = a.shape
    H = MLA_HEADS
    down = a @ w_down
    c_q = rms_norm(down[..., :MLA_Q_LORA], g_cq)
    c_kv = rms_norm(down[..., MLA_Q_LORA:MLA_Q_LORA + MLA_KV_LORA], g_ckv)
    k_pe = down[..., MLA_Q_LORA + MLA_KV_LORA:]
    q = (c_q @ w_uq).reshape(B, L, H, MLA_QK)
    kv = (c_kv @ w_ukv).reshape(B, L, H, MLA_NOPE + MLA_V)
    k = jnp.concatenate(
        [kv[..., :MLA_NOPE], jnp.broadcast_to(k_pe[:, :, None, :], (B, L, H, MLA_ROPE))], axis=-1)
    v = kv[..., MLA_NOPE:]
    q = rms_norm(q, g_qhead)
    k = rms_norm(k, g_khead)
    q = jnp.concatenate([q[..., :MLA_NOPE], apply_rope(q[..., MLA_NOPE:], cos, sin)], axis=-1)
    k = jnp.concatenate([k[..., :MLA_NOPE], apply_rope(k[..., MLA_NOPE:], cos, sin)], axis=-1)
    scale = MLA_QK ** -0.5

    def attend(qb, kb, vb, mask):
        s = jnp.einsum("bqhd,bkhd->bhqk", qb, kb).astype(F32) * scale
        s = jnp.where(mask[None, None], s, NEG_INF)
        p = jax.nn.softmax(s, axis=-1).astype(vb.dtype)
        return jnp.einsum("bhqk,bkhd->bqhd", p, vb)

    o_meta = attend(q[:, :N_META], k[:, :N_META], v[:, :N_META], jnp.ones((N_META, N_META), bool))
    n_real = L - N_META
    n_blk = n_real // Q_BLOCK
    q_blk = q[:, N_META:].reshape(B, n_blk, Q_BLOCK, H, MLA_QK).transpose(1, 0, 2, 3, 4)
    cid_blk = cid[N_META:].reshape(n_blk, Q_BLOCK)
    o_real = lax.map(lambda args: attend(args[0], k, v, cid[None, :] <= args[1][:, None]),
                     (q_blk, cid_blk))
    o_real = o_real.transpose(1, 0, 2, 3, 4).reshape(B, n_real, H, MLA_V)
    o = jnp.concatenate([o_meta, o_real], axis=1).reshape(B, L, H * MLA_V)
    return o @ w_o


def gla_chunkwise(q, k, v, log_f, chunk):
    B, L, H, DK = q.shape
    DV = v.shape[-1]
    n_pad = (-L) % chunk
    NC = (L + n_pad) // chunk

    def to_chunks(t):
        return pad_front(t, n_pad).reshape(B, NC, chunk, H, t.shape[-1]).transpose(1, 0, 3, 2, 4)

    qc, kc, vc, gc = map(to_chunks, (q, k, v, log_f))
    G = jnp.cumsum(gc, axis=3)
    G_last = G[..., -1:, :]
    q_dec = qc * jnp.exp(G)
    k_inv = kc * jnp.exp(-G)
    k_tail = kc * jnp.exp(G_last - G)
    causal = jnp.tril(jnp.ones((chunk, chunk), bool))
    attn = jnp.where(causal, jnp.einsum("nbhid,nbhjd->nbhij", q_dec, k_inv), 0.0)
    o_intra = jnp.einsum("nbhij,nbhjv->nbhiv", attn, vc)

    def step(state, xs):
        qd, kt, vv, gl = xs
        o_inter = jnp.einsum("bhid,bhdv->bhiv", qd, state)
        state = state * jnp.exp(gl)[:, :, 0, :, None] + jnp.einsum("bhjd,bhjv->bhdv", kt, vv)
        return state, o_inter

    _, o_inter = lax.scan(step, jnp.zeros((B, H, DK, DV), F32), (q_dec, k_tail, vc, G_last))
    o = (o_intra + o_inter).transpose(1, 0, 3, 2, 4).reshape(B, NC * chunk, H, DV)
    return o[:, n_pad:]


def hgrn2_mixer(a, w_in, lb, g_o, w_o):
    B, L, _ = a.shape
    H = HGRN_HEADS
    q, f, i_in, g = jnp.split(a @ w_in, 4, axis=-1)
    q = jax.nn.silu(q.astype(F32))
    forget = lb + (1.0 - lb) * jax.nn.sigmoid(f.astype(F32))
    log_f = jnp.log(forget)
    k = 1.0 - forget
    shp = (B, L, H, HGRN_DK)
    o = gla_chunkwise(q.reshape(shp), k.reshape(shp),
                      i_in.astype(F32).reshape(B, L, H, HGRN_DV), log_f.reshape(shp), HGRN_CHUNK)
    o = rms_norm(o, g_o).reshape(B, L, H * HGRN_DV) * jax.nn.silu(g.astype(F32))
    return o.astype(a.dtype) @ w_o


def _complex_affine_combine(e1, e2):
    a1r, a1i, b1r, b1i = e1
    a2r, a2i, b2r, b2i = e2
    return (a2r * a1r - a2i * a1i,
            a2r * a1i + a2i * a1r,
            a2r * b1r - a2i * b1i + b2r,
            a2r * b1i + a2i * b1r + b2i)


def s5_mixer(a, lam_re, lam_im, log_dt, b_re, b_im, c_re, c_im, d_skip, w_glu):
    B, L, D = a.shape
    G, P, K, C = S5_GROUPS, S5_STATE, S5_GROUP, CHUNK
    lam_re, lam_im = lam_re.astype(F32), lam_im.astype(F32)
    b_re, b_im = b_re.astype(F32), b_im.astype(F32)
    c_re, c_im = c_re.astype(F32), c_im.astype(F32)
    dt = jnp.exp(log_dt.astype(F32))[:, None]
    mag = jnp.exp(lam_re * dt)
    abar_re = mag * jnp.cos(lam_im * dt)
    abar_im = mag * jnp.sin(lam_im * dt)
    den = lam_re * lam_re + lam_im * lam_im
    zoh_re = ((abar_re - 1.0) * lam_re + abar_im * lam_im) / den
    zoh_im = (abar_im * lam_re - (abar_re - 1.0) * lam_im) / den
    bbar_re = zoh_re[..., None] * b_re - zoh_im[..., None] * b_im
    bbar_im = zoh_re[..., None] * b_im + zoh_im[..., None] * b_re
    steps = (jnp.arange(C, dtype=F32) + 1.0)[:, None, None] * dt[None]
    pmag = jnp.exp(lam_re[None] * steps)
    pow_re = pmag * jnp.cos(lam_im[None] * steps)
    pow_im = pmag * jnp.sin(lam_im[None] * steps)

    n_pad = (-L) % C
    NC = (L + n_pad) // C
    u = pad_front(a.astype(F32), n_pad).reshape(B, NC, C, G, K).transpose(1, 0, 2, 3, 4)

    def step(carry, uc):
        xr0, xi0 = carry
        bu_re = jnp.einsum("bcgk,gpk->bcgp", uc, bbar_re)
        bu_im = jnp.einsum("bcgk,gpk->bcgp", uc, bbar_im)
        ar = jnp.broadcast_to(abar_re, bu_re.shape)
        ai = jnp.broadcast_to(abar_im, bu_im.shape)
        _, _, xr, xi = lax.associative_scan(_complex_affine_combine, (ar, ai, bu_re, bu_im), axis=1)
        xr = xr + pow_re * xr0[:, None] - pow_im * xi0[:, None]
        xi = xi + pow_re * xi0[:, None] + pow_im * xr0[:, None]
        y = jnp.einsum("bcgp,gkp->bcgk", xr, c_re) - jnp.einsum("bcgp,gkp->bcgk", xi, c_im)
        return (xr[:, -1], xi[:, -1]), y

    zeros = jnp.zeros((B, G, P), F32)
    _, y = lax.scan(step, (zeros, zeros), u)
    y = y.transpose(1, 0, 2, 3, 4).reshape(B, NC * C, D)[:, n_pad:]
    y = jax.nn.gelu(y + d_skip.astype(F32) * a.astype(F32))
    val, gate = jnp.split(y.astype(a.dtype) @ w_glu, 2, axis=-1)
    return val * jax.nn.sigmoid(gate)


def retention_mixer(a, w_in, gn_g, w_o, cos, sin):
    B, L, _ = a.shape
    H, DK, DV, C = RET_HEADS, RET_DK, RET_DV, CHUNK
    qk_w = H * DK
    q, k, v, g = jnp.split(a @ w_in, [qk_w, 2 * qk_w, 2 * qk_w + H * DV], axis=-1)
    q = apply_rope(q.reshape(B, L, H, DK), cos, sin).astype(F32)
    k = apply_rope(k.reshape(B, L, H, DK), cos, sin).astype(F32) * (DK ** -0.5)
    v = v.reshape(B, L, H, DV).astype(F32)
    n_pad = (-L) % C
    NC = (L + n_pad) // C

    def to_chunks(t):
        return pad_front(t, n_pad).reshape(B, NC, C, H, t.shape[-1]).transpose(1, 0, 3, 2, 4)

    qc, kc, vc = map(to_chunks, (q, k, v))
    log_gamma = jnp.log(1.0 - jnp.exp2(-5.0 - jnp.arange(H, dtype=F32)))
    pos = jnp.arange(C, dtype=F32)
    diff = pos[:, None] - pos[None, :]
    decay = jnp.where(diff >= 0, jnp.exp(diff[None] * log_gamma[:, None, None]), 0.0)
    scores = jnp.einsum("nbhid,nbhjd->nbhij", qc, kc) * decay
    o_intra = jnp.einsum("nbhij,nbhjv->nbhiv", scores, vc)
    q_decay = jnp.exp((pos[None, :] + 1.0) * log_gamma[:, None])[..., None]
    k_decay = jnp.exp((C - 1.0 - pos[None, :]) * log_gamma[:, None])[..., None]
    chunk_decay = jnp.exp(C * log_gamma)[:, None, None]

    def step(state, xs):
        qi, ki, vi = xs
        o_inter = jnp.einsum("bhid,bhdv->bhiv", qi * q_decay, state)
        state = state * chunk_decay + jnp.einsum("bhjd,bhjv->bhdv", ki * k_decay, vi)
        return state, o_inter

    _, o_inter = lax.scan(step, jnp.zeros((B, H, DK, DV), F32), (qc, kc, vc))
    o = (o_intra + o_inter).transpose(1, 0, 3, 2, 4).reshape(B, NC * C, H, DV)[:, n_pad:]
    mu = jnp.mean(o, axis=-1, keepdims=True)
    var = jnp.mean(jnp.square(o - mu), axis=-1, keepdims=True)
    o = ((o - mu) * lax.rsqrt(var + EPS)).reshape(B, L, H * DV)
    o = o * gn_g.astype(F32) * jax.nn.silu(g.astype(F32))
    return o.astype(a.dtype) @ w_o


def conv_glu_ffn(a, w_up, conv_w, conv_b, w_down):
    u = a @ w_up
    u = lax.conv_general_dilated(
        u, conv_w[:, None, :].astype(u.dtype), window_strides=(1,),
        padding=[(CONV_WIDTH - 1, 0)], dimension_numbers=("NWC", "WIO", "NWC"),
        feature_group_count=u.shape[-1]) + conv_b.astype(u.dtype)
    gate, val = jnp.split(u, 2, axis=-1)
    return (jax.nn.silu(gate) * val) @ w_down


def setup_inputs(seed: int = 0) -> dict:
    key = jax.random.key(seed)
    ks = iter(jax.random.split(key, 40))
    D = D_MODEL

    def nrm(shape, scale):
        return jax.random.normal(next(ks), shape, F32) * scale

    def gain(shape):
        return 1.0 + 0.02 * jax.random.normal(next(ks), shape, F32)

    n_a, n_b, n_c, n_d = N_MLA_L, N_HGRN_L, N_S5_L, N_RET_L
    return {
        "x": nrm((BATCH, SEQ, D), 1.0),
        "meta_tokens": nrm((N_META, D), 1.0),
        "norm_mix_g": gain((DEPTH, D)),
        "norm_ffn_g": gain((DEPTH, D)),
        "mla_w_down": nrm((n_a, D, MLA_Q_LORA + MLA_KV_LORA + MLA_ROPE), D ** -0.5),
        "mla_cq_norm_g": gain((n_a, MLA_Q_LORA)),
        "mla_ckv_norm_g": gain((n_a, MLA_KV_LORA)),
        "mla_w_uq": nrm((n_a, MLA_Q_LORA, MLA_HEADS * MLA_QK), MLA_Q_LORA ** -0.5),
        "mla_w_ukv": nrm((n_a, MLA_KV_LORA, MLA_HEADS * (MLA_NOPE + MLA_V)), MLA_KV_LORA ** -0.5),
        "mla_q_head_g": gain((n_a, MLA_QK)),
        "mla_k_head_g": gain((n_a, MLA_QK)),
        "mla_w_o": nrm((n_a, MLA_HEADS * MLA_V, D), (MLA_HEADS * MLA_V) ** -0.5),
        "hgrn_w_in": nrm((n_b, D, 4 * D), D ** -0.5),
        "hgrn_lb_logits": nrm((DEPTH, HGRN_HEADS * HGRN_DK), 0.1),
        "hgrn_o_norm_g": gain((n_b, HGRN_DV)),
        "hgrn_w_o": nrm((n_b, D, D), D ** -0.5),
        "s5_lam_re": -0.5 + nrm((n_c, S5_GROUPS, S5_STATE), 0.01),
        "s5_lam_im": jnp.pi * jnp.arange(S5_STATE, dtype=F32)[None, None, :] + nrm((n_c, S5_GROUPS, S5_STATE), 0.01),
        "s5_log_dt": jax.random.uniform(next(ks), (n_c, S5_GROUPS), F32,
                                        minval=math.log(S5_DT_MIN), maxval=math.log(S5_DT_MAX)),
        "s5_b_re": nrm((n_c, S5_GROUPS, S5_STATE, S5_GROUP), (2 * S5_GROUP) ** -0.5),
        "s5_b_im": nrm((n_c, S5_GROUPS, S5_STATE, S5_GROUP), (2 * S5_GROUP) ** -0.5),
        "s5_c_re": nrm((n_c, S5_GROUPS, S5_GROUP, S5_STATE), (2 * S5_STATE) ** -0.5),
        "s5_c_im": nrm((n_c, S5_GROUPS, S5_GROUP, S5_STATE), (2 * S5_STATE) ** -0.5),
        "s5_d": nrm((n_c, D), 1.0),
        "s5_w_glu": nrm((n_c, D, 2 * D), D ** -0.5),
        "ret_w_in": nrm((n_d, D, 2 * RET_HEADS * RET_DK + 2 * RET_HEADS * RET_DV), D ** -0.5),
        "ret_gn_g": gain((n_d, RET_HEADS * RET_DV)),
        "ret_w_o": nrm((n_d, RET_HEADS * RET_DV, D), (RET_HEADS * RET_DV) ** -0.5),
        "ffn_w_up": nrm((DEPTH, D, 2 * FFN_HIDDEN), D ** -0.5),
        "ffn_conv_w": nrm((DEPTH, CONV_WIDTH, 2 * FFN_HIDDEN), CONV_WIDTH ** -0.5),
        "ffn_conv_b": nrm((DEPTH, 2 * FFN_HIDDEN), 0.01),
        "ffn_w_down": nrm((DEPTH, FFN_HIDDEN, D), FFN_HIDDEN ** -0.5),
    }


def reference(x, meta_tokens, norm_mix_g, norm_ffn_g,
              mla_w_down, mla_cq_norm_g, mla_ckv_norm_g, mla_w_uq, mla_w_ukv, mla_q_head_g, mla_k_head_g, mla_w_o,
              hgrn_w_in, hgrn_lb_logits, hgrn_o_norm_g, hgrn_w_o,
              s5_lam_re, s5_lam_im, s5_log_dt, s5_b_re, s5_b_im, s5_c_re, s5_c_im, s5_d, s5_w_glu,
              ret_w_in, ret_gn_g, ret_w_o,
              ffn_w_up, ffn_conv_w, ffn_conv_b, ffn_w_down):
    B = x.shape[0]
    L = x.shape[1] + N_META
    h = jnp.concatenate(
        [jnp.broadcast_to(meta_tokens[None].astype(x.dtype), (B, N_META, D_MODEL)), x], axis=1)
    cid = chunk_ids(L)
    cos_a, sin_a = rope_tables(L, MLA_ROPE)
    cos_d, sin_d = rope_tables(L, RET_DK)
    lb_cum = jnp.cumsum(jax.nn.softmax(hgrn_lb_logits.astype(F32), axis=0), axis=0)
    lb_all = lb_cum - lb_cum[0:1]

    for i in range(DEPTH):
        m, j = i % N_MIXERS, i // N_MIXERS
        a = rms_norm(h, norm_mix_g[i])
        if m == 0:
            y = mla_mixer(a, mla_w_down[j], mla_cq_norm_g[j], mla_ckv_norm_g[j], mla_w_uq[j], mla_w_ukv[j],
                          mla_q_head_g[j], mla_k_head_g[j], mla_w_o[j], cos_a, sin_a, cid)
        elif m == 1:
            y = hgrn2_mixer(a, hgrn_w_in[j], lb_all[i], hgrn_o_norm_g[j], hgrn_w_o[j])
        elif m == 2:
            y = s5_mixer(a, s5_lam_re[j], s5_lam_im[j], s5_log_dt[j], s5_b_re[j], s5_b_im[j],
                         s5_c_re[j], s5_c_im[j], s5_d[j], s5_w_glu[j])
        else:
            y = retention_mixer(a, ret_w_in[j], ret_gn_g[j], ret_w_o[j], cos_d, sin_d)
        h = h + y.astype(h.dtype)
        f = conv_glu_ffn(rms_norm(h, norm_ffn_g[i]), ffn_w_up[i], ffn_conv_w[i], ffn_conv_b[i], ffn_w_down[i])
        h = h + f.astype(h.dtype)
    return h[:, N_META:]
```

```python
import functools
import math

import jax
import jax.numpy as jnp
from jax import lax
from jax.experimental import pallas as pl
from jax.experimental.pallas import tpu as pltpu

F32 = jnp.float32
BF16 = jnp.bfloat16

N_META = 16
EPS = 1e-6
NEG_INF = -1e30
ROPE_BASE = 10000.0
MLA_HEADS = 8
MLA_NOPE = 128
MLA_ROPE = 64
MLA_V = 128
MLA_QK = MLA_NOPE + MLA_ROPE
MLA_Q_LORA = 384
MLA_KV_LORA = 256
HGRN_HEADS = 8
HGRN_DK = 128
S5_GROUP = 16
S5_STATE = 64
RET_HEADS = 4

LANES = 128
PAD = LANES - N_META
FRONT = PAD + N_META
ROW_TILE = 384
ATT_BLOCK = 256
HGRN_CHUNK = 64
HGRN_REF = HGRN_CHUNK // 2 - 1
RET_CHUNK = 128
S5_BLOCK = 16
S5_GROUP_CHUNK = 4
FFN_COLS = 256
VMEM_LIMIT = 56 * 1024 * 1024


def _cparams(*sem):
    return pltpu.CompilerParams(dimension_semantics=sem, vmem_limit_bytes=VMEM_LIMIT)


def _resident(shape):
    nd = len(shape)
    return pl.BlockSpec(shape, lambda *_: (0,) * nd, pipeline_mode=pl.Buffered(1))


def _rms(x, g):
    return x * lax.rsqrt(jnp.mean(x * x, axis=-1, keepdims=True) + EPS) * g


def _dot(a, b):
    return jnp.dot(a, b, preferred_element_type=F32)


def _dot_nt(a, b):
    return lax.dot_general(a, b, (((1,), (1,)), ((), ())), preferred_element_type=F32)


def _dot_tn(a, b):
    return lax.dot_general(a, b, (((0,), (0,)), ((), ())), preferred_element_type=F32)


def _sigmoid(x):
    return 1.0 / (1.0 + jnp.exp(-x))


def _silu(x):
    return x * _sigmoid(x)


def _gelu_tanh(x):
    return 0.5 * x * (1.0 + jnp.tanh(math.sqrt(2.0 / math.pi) * (x + 0.044715 * (x * x * x))))


def _tile_valid(t, tm):
    return t * tm + lax.broadcasted_iota(jnp.int32, (tm, 1), 0) >= PAD


def _ffn_tail(x, t, g_ref, wup_ref, cw_ref, cb_ref, wdn_ref, o_ref, carry_ref, act_ref):
    tm = x.shape[0]
    hidden = wdn_ref.shape[0]
    valid = _tile_valid(t, tm)
    x = jnp.where(valid, x, 0.0)
    a = _rms(x, g_ref[...]).astype(BF16)
    rid = lax.broadcasted_iota(jnp.int32, (tm, 1), 0)

    @pl.when(t == 0)
    def _():
        carry_ref[...] = jnp.zeros_like(carry_ref)

    def conv_slice(col0):
        cols = slice(col0, col0 + FFN_COLS)
        u = _dot(a, wup_ref[:, cols])
        prev = carry_ref[:, cols]
        p0, p1 = prev[6:7], prev[7:8]
        u1 = jnp.where(rid == 0, p1, pltpu.roll(u, 1, 0))
        u2 = jnp.where(rid == 0, p0, jnp.where(rid == 1, p1, pltpu.roll(u, 2, 0)))
        carry_ref[:, cols] = u[tm - 8:tm]
        w = cw_ref[:, cols]
        return w[0:1] * u2 + w[1:2] * u1 + w[2:3] * u + cb_ref[:, cols]

    for c in range(hidden // FFN_COLS):
        gate = conv_slice(c * FFN_COLS)
        val = conv_slice(hidden + c * FFN_COLS)
        act_ref[:, c * FFN_COLS:(c + 1) * FFN_COLS] = (_silu(gate) * val).astype(BF16)
    y = _dot(act_ref[...], wdn_ref[...])
    o_ref[...] = jnp.where(valid, x + y, 0.0)


def _ffn_proj_kernel(h_ref, o_in_ref, wo_ref, g_ref, wup_ref, cw_ref, cb_ref, wdn_ref,
                     o_ref, carry_ref, act_ref):
    t = pl.program_id(1)
    x = h_ref[...] + _dot(o_in_ref[...], wo_ref[...])
    _ffn_tail(x, t, g_ref, wup_ref, cw_ref, cb_ref, wdn_ref, o_ref, carry_ref, act_ref)


def _ffn_s5_kernel(h_ref, y_ref, gmix_ref, d_ref, wglu_ref, g_ref, wup_ref, cw_ref, cb_ref, wdn_ref,
                   o_ref, carry_ref, act_ref):
    t = pl.program_id(1)
    h = h_ref[...]
    d_model = h.shape[1]
    a = _rms(h, gmix_ref[...])
    z = _gelu_tanh(y_ref[...].astype(F32) + d_ref[...] * a).astype(BF16)
    val = _dot(z, wglu_ref[:, :d_model])
    gate = _dot(z, wglu_ref[:, d_model:])
    x = h + val * _sigmoid(gate)
    _ffn_tail(x, t, g_ref, wup_ref, cw_ref, cb_ref, wdn_ref, o_ref, carry_ref, act_ref)


def _ffn_call(kernel, h, lead_inputs, lead_specs, ffn_w):
    b, lp, d = h.shape
    g, wup, cw, cb, wdn = ffn_w
    hidden = wdn.shape[0]
    tm = ROW_TILE
    row_spec = lambda width: pl.BlockSpec((None, tm, width), lambda i, t: (i, t, 0))
    return pl.pallas_call(
        kernel,
        grid=(b, lp // tm),
        in_specs=[row_spec(d)] + lead_specs(row_spec) + [
            _resident(g.shape), _resident(wup.shape), _resident(cw.shape), _resident(cb.shape),
            _resident(wdn.shape)],
        out_specs=row_spec(d),
        out_shape=jax.ShapeDtypeStruct((b, lp, d), F32),
        scratch_shapes=[pltpu.VMEM((8, 2 * hidden), F32), pltpu.VMEM((tm, hidden), BF16)],
        compiler_params=_cparams("arbitrary", "arbitrary"),
    )(h, *lead_inputs, g, wup, cw, cb, wdn)


def _ffn_after_proj(h, o_in, wo, ffn_w):
    return _ffn_call(_ffn_proj_kernel, h, (o_in, wo),
                     lambda row_spec: [row_spec(o_in.shape[-1]), _resident(wo.shape)], ffn_w)


def _ffn_after_s5(h, y, gmix, dskip, wglu, ffn_w):
    return _ffn_call(_ffn_s5_kernel, h, (y, gmix, dskip, wglu),
                     lambda row_spec: [row_spec(y.shape[-1]), _resident(gmix.shape), _resident(dskip.shape),
                                       _resident(wglu.shape)], ffn_w)


def _rope_swap(r):
    lane = lax.broadcasted_iota(jnp.int32, r.shape, 1)
    half = MLA_ROPE // 2
    return jnp.where(lane < half, pltpu.roll(r, LANES - half, 1), pltpu.roll(r, half, 1))


def _mla_qkv_kernel(h_ref, gmix_ref, wd_ref, gcq_ref, gckv_ref, wuq_ref, wukv_ref,
                    gqn_ref, gqr_ref, gkn_ref, gkr_ref, cos_ref, sin_ref, q_ref, k_ref, v_ref):
    a = _rms(h_ref[...], gmix_ref[...]).astype(BF16)
    down = _dot(a, wd_ref[...])
    kv_lo = MLA_Q_LORA + MLA_KV_LORA
    cq = _rms(down[:, :MLA_Q_LORA], gcq_ref[...]).astype(BF16)
    ckv = _rms(down[:, MLA_Q_LORA:kv_lo], gckv_ref[...]).astype(BF16)
    kpe = down[:, kv_lo:kv_lo + LANES]
    q = _dot(cq, wuq_ref[...])
    kv = _dot(ckv, wukv_ref[...])
    cos, sin = cos_ref[...], sin_ref[...]

    def rope(r):
        return r * cos + _rope_swap(r) * sin

    ss_pe = jnp.sum(kpe * kpe, axis=-1, keepdims=True)
    k_rope = rope(kpe * gkr_ref[...])
    scale = MLA_QK ** -0.5
    hw = 2 * LANES
    for hd in range(MLA_HEADS):
        qn = q[:, hd * hw:hd * hw + LANES]
        qr = q[:, hd * hw + LANES:(hd + 1) * hw]
        inv_q = lax.rsqrt(jnp.sum(qn * qn + qr * qr, axis=-1, keepdims=True) / MLA_QK + EPS) * scale
        q_ref[:, hd * hw:hd * hw + LANES] = (qn * inv_q * gqn_ref[...]).astype(BF16)
        q_ref[:, hd * hw + LANES:(hd + 1) * hw] = (rope(qr * gqr_ref[...]) * inv_q).astype(BF16)
        kn = kv[:, hd * hw:hd * hw + LANES]
        inv_k = lax.rsqrt((jnp.sum(kn * kn, axis=-1, keepdims=True) + ss_pe) / MLA_QK + EPS)
        k_ref[:, hd * hw:hd * hw + LANES] = (kn * inv_k * gkn_ref[...]).astype(BF16)
        k_ref[:, hd * hw + LANES:(hd + 1) * hw] = (k_rope * inv_k).astype(BF16)
        v_ref[:, hd * LANES:(hd + 1) * LANES] = kv[:, hd * hw + LANES:(hd + 1) * hw].astype(BF16)


def _attn_kernel(q_ref, k_ref, v_ref, o_ref):
    lp = q_ref.shape[0]
    blk = ATT_BLOCK
    n_q = (lp - FRONT) // blk
    k0, v0 = k_ref[0:FRONT], v_ref[0:FRONT]
    key_ok0 = lax.broadcasted_iota(jnp.int32, (1, FRONT), 1) >= PAD
    chunk = FRONT // 2
    diag_ok = (lax.broadcasted_iota(jnp.int32, (blk, blk), 0) // chunk
               >= lax.broadcasted_iota(jnp.int32, (blk, blk), 1) // chunk)

    def first_block(q):
        s = jnp.where(key_ok0, _dot_nt(q, k0), NEG_INF)
        m = jnp.max(s, axis=-1, keepdims=True)
        p = jnp.exp(s - m)
        return m, jnp.sum(p, axis=-1, keepdims=True), _dot(p.astype(BF16), v0)

    def update(carry, s, vb):
        m, l, acc = carry
        m_new = jnp.maximum(m, jnp.max(s, axis=-1, keepdims=True))
        alpha = jnp.exp(m - m_new)
        p = jnp.exp(s - m_new)
        return (m_new, alpha * l + jnp.sum(p, axis=-1, keepdims=True),
                alpha * acc + _dot(p.astype(BF16), vb))

    _, l, acc = first_block(q_ref[0:FRONT])
    o_ref[0:FRONT] = (acc / l).astype(o_ref.dtype)

    def q_body(i, _):
        r0 = pl.multiple_of(FRONT + i * blk, LANES)
        q = q_ref[pl.ds(r0, blk)]

        def kv_body(j, carry):
            c0 = pl.multiple_of(FRONT + j * blk, LANES)
            return update(carry, _dot_nt(q, k_ref[pl.ds(c0, blk)]), v_ref[pl.ds(c0, blk)])

        carry = lax.fori_loop(0, i, kv_body, first_block(q))
        s = jnp.where(diag_ok, _dot_nt(q, k_ref[pl.ds(r0, blk)]), NEG_INF)
        _, l, acc = update(carry, s, v_ref[pl.ds(r0, blk)])
        o_ref[pl.ds(r0, blk)] = (acc / l).astype(o_ref.dtype)
        return 0

    lax.fori_loop(0, n_q, q_body, 0)


def _mla_rope_tables(lp):
    half = MLA_ROPE // 2
    pos = jnp.maximum(jnp.arange(lp, dtype=F32) - PAD, 0.0)
    inv_freq = 1.0 / (ROPE_BASE ** (jnp.arange(0, MLA_ROPE, 2, dtype=F32) / MLA_ROPE))
    ang = pos[:, None] * inv_freq[None, :]
    c, s = jnp.cos(ang), jnp.sin(ang)
    z = jnp.zeros((lp, LANES - MLA_ROPE), F32)
    return jnp.concatenate([c, c, z], axis=1), jnp.concatenate([-s, s, z], axis=1)


def _mla_mixer(h, g_mix, w_down, g_cq, g_ckv, w_uq, w_ukv, g_qhead, g_khead):
    b, lp, d = h.shape
    nh, hw = MLA_HEADS, 2 * LANES
    kv_lo = MLA_Q_LORA + MLA_KV_LORA
    wd = jnp.pad(w_down, ((0, 0), (0, kv_lo + LANES - w_down.shape[1]))).astype(BF16)
    wuq = jnp.pad(w_uq.reshape(MLA_Q_LORA, nh, MLA_QK), ((0, 0), (0, 0), (0, hw - MLA_QK)))
    wuq = wuq.reshape(MLA_Q_LORA, nh * hw).astype(BF16)
    wukv = w_ukv.astype(BF16)
    zpad = jnp.zeros((LANES - MLA_ROPE,), F32)
    gqn, gkn = g_qhead[None, :MLA_NOPE], g_khead[None, :MLA_NOPE]
    gqr = jnp.concatenate([g_qhead[MLA_NOPE:], zpad])[None]
    gkr = jnp.concatenate([g_khead[MLA_NOPE:], zpad])[None]
    cos, sin = _mla_rope_tables(lp)
    tm = ROW_TILE
    row_spec = lambda width: pl.BlockSpec((None, tm, width), lambda i, t: (i, t, 0))
    tab_spec = pl.BlockSpec((tm, LANES), lambda i, t: (t, 0))
    small = [g_mix[None], wd, g_cq[None], g_ckv[None], wuq, wukv, gqn, gqr, gkn, gkr]
    q, k, v = pl.pallas_call(
        _mla_qkv_kernel,
        grid=(b, lp // tm),
        in_specs=[row_spec(d)] + [_resident(w.shape) for w in small] + [tab_spec, tab_spec],
        out_specs=[row_spec(nh * hw), row_spec(nh * hw), row_spec(nh * MLA_V)],
        out_shape=[jax.ShapeDtypeStruct((b, lp, nh * hw), BF16), jax.ShapeDtypeStruct((b, lp, nh * hw), BF16),
                   jax.ShapeDtypeStruct((b, lp, nh * MLA_V), BF16)],
        compiler_params=_cparams("parallel", "parallel"),
    )(h, *small, cos, sin)
    head_spec = lambda width: pl.BlockSpec((None, lp, width), lambda i, j: (i, 0, j))
    return pl.pallas_call(
        _attn_kernel,
        grid=(b, nh),
        in_specs=[head_spec(hw), head_spec(hw), head_spec(MLA_V)],
        out_specs=head_spec(MLA_V),
        out_shape=jax.ShapeDtypeStruct((b, lp, nh * MLA_V), BF16),
        compiler_params=_cparams("parallel", "parallel"),
    )(q, k, v)


def _hgrn_kernel(h_ref, gmix_ref, win_ref, lb_ref, go_ref, tri_ref, o_ref,
                 state_ref, q_s, k_s, v_s, lf_s, gate_s):
    t = pl.program_id(1)
    tm, d = h_ref.shape
    dk = HGRN_DK
    c = HGRN_CHUNK

    @pl.when(t == 0)
    def _():
        state_ref[...] = jnp.zeros_like(state_ref)

    a = _rms(h_ref[...], gmix_ref[...]).astype(BF16)
    lb = lb_ref[...]
    q_s[...] = _silu(_dot(a, win_ref[:, 0:d]))
    forget = lb + (1.0 - lb) * _sigmoid(_dot(a, win_ref[:, d:2 * d]))
    lf_s[...] = jnp.log(forget)
    k_s[...] = 1.0 - forget
    v_s[...] = _dot(a, win_ref[:, 2 * d:3 * d]).astype(BF16)
    gate_s[...] = _silu(_dot(a, win_ref[:, 3 * d:4 * d]))
    causal = (lax.broadcasted_iota(jnp.int32, (c, c), 0) >= lax.broadcasted_iota(jnp.int32, (c, c), 1))
    tri = tri_ref[...]

    def chunk_body(ci, _):
        rows = pl.ds(pl.multiple_of(ci * c, c), c)
        lf = lf_s[rows]
        lf_hi = lf.astype(BF16)
        lf_lo = (lf - lf_hi.astype(F32)).astype(BF16)
        cum = _dot(tri, lf_hi) + _dot(tri, lf_lo)
        ref_row = cum[HGRN_REF:HGRN_REF + 1]
        last = cum[c - 1:c]
        q, k, v = q_s[rows], k_s[rows], v_s[rows]
        q_rel = (q * jnp.exp(cum - ref_row)).astype(BF16)
        k_rel = (k * jnp.exp(ref_row - cum)).astype(BF16)
        q_dec = (q * jnp.exp(cum)).astype(BF16)
        k_tail = (k * jnp.exp(last - cum)).astype(BF16)
        decay = jnp.exp(last)
        gate = gate_s[rows]
        for hd in range(d // dk):
            sl = slice(hd * dk, (hd + 1) * dk)
            st = state_ref[hd]
            attn = jnp.where(causal, _dot_nt(q_rel[:, sl], k_rel[:, sl]), 0.0).astype(BF16)
            o = _dot(attn, v[:, sl]) + _dot_nt(q_dec[:, sl], st.astype(BF16))
            state_ref[hd] = st * decay[:, sl] + _dot_tn(v[:, sl], k_tail[:, sl])
            o = _rms(o, go_ref[...]) * gate[:, sl]
            o_ref[rows, sl] = o.astype(o_ref.dtype)
        return 0

    lax.fori_loop(0, tm // c, chunk_body, 0)


def _hgrn_mixer(h, g_mix, w_in, lb, g_o):
    b, lp, d = h.shape
    tm = ROW_TILE
    c = HGRN_CHUNK
    tri = (jnp.arange(c)[:, None] >= jnp.arange(c)[None, :]).astype(BF16)
    row_spec = pl.BlockSpec((None, tm, d), lambda i, t: (i, t, 0))
    small = [g_mix[None], w_in.astype(BF16), lb[None], g_o[None], tri]
    return pl.pallas_call(
        _hgrn_kernel,
        grid=(b, lp // tm),
        in_specs=[row_spec] + [_resident(w.shape) for w in small],
        out_specs=row_spec,
        out_shape=jax.ShapeDtypeStruct((b, lp, d), BF16),
        scratch_shapes=[pltpu.VMEM((d // HGRN_DK, d // HGRN_HEADS, HGRN_DK), F32),
                        pltpu.VMEM((tm, d), F32), pltpu.VMEM((tm, d), F32), pltpu.VMEM((tm, d), BF16),
                        pltpu.VMEM((tm, d), F32), pltpu.VMEM((tm, d), F32)],
        compiler_params=_cparams("arbitrary", "arbitrary"),
    )(h, *small)


def _norm_kernel(h_ref, g_ref, o_ref):
    o_ref[...] = _rms(h_ref[...], g_ref[...]).astype(o_ref.dtype)


def _s5_kernel(u_ref, mw_ref, toe_ref, xm_ref, a1_ref, a2_ref, y_ref, w_s, x_s):
    gc, rows, width = u_ref.shape
    nb = a1_ref.shape[0] // 2
    for g in range(gc):
        w_s[:, g * width:(g + 1) * width] = _dot(u_ref[g], mw_ref[g])
    a1, a2 = a1_ref[...], a2_ref[...]
    top = lax.broadcasted_iota(jnp.int32, (2 * nb, 1), 0) < nb

    def swap_halves(z):
        parts = []
        for g in range(gc):
            parts += [z[:, g * width + width // 2:(g + 1) * width], z[:, g * width:g * width + width // 2]]
        return jnp.concatenate(parts, axis=1)

    def step(z, w):
        zr = pltpu.roll(z, nb, 0)
        return zr, a1 * zr + a2 * swap_halves(zr) + w

    def pair_body(v, z):
        rows8 = pl.ds(pl.multiple_of(v * 2 * nb, 2 * nb), 2 * nb)
        w = w_s[rows8]
        start_even, z = step(z, w)
        start_odd, z = step(z, w)
        x_s[rows8] = jnp.where(top, start_even, start_odd)
        return z

    lax.fori_loop(0, rows // (2 * nb), pair_body, jnp.zeros((2 * nb, gc * width), F32))
    half = width // 2
    for g in range(gc):
        x_start = x_s[:, g * width:g * width + half].astype(BF16)
        y_ref[g] = (_dot(u_ref[g], toe_ref[g]) + _dot(x_start, xm_ref[g])).astype(y_ref.dtype)


def _s5_operators(lam_re, lam_im, log_dt, b_re, b_im, c_re, c_im, nb):
    hp = lax.Precision.HIGHEST
    t = S5_BLOCK
    dt = jnp.exp(log_dt)[:, None]
    mag = jnp.exp(lam_re * dt)
    abar_re, abar_im = mag * jnp.cos(lam_im * dt), mag * jnp.sin(lam_im * dt)
    den = lam_re * lam_re + lam_im * lam_im
    zoh_re = ((abar_re - 1.0) * lam_re + abar_im * lam_im) / den
    zoh_im = (abar_im * lam_re - (abar_re - 1.0) * lam_im) / den
    bb_re = zoh_re[..., None] * b_re - zoh_im[..., None] * b_im
    bb_im = zoh_re[..., None] * b_im + zoh_im[..., None] * b_re
    steps = jnp.arange(t + 1, dtype=F32)[:, None, None] * dt[None]
    pmag = jnp.exp(lam_re[None] * steps)
    pw_re, pw_im = pmag * jnp.cos(lam_im[None] * steps), pmag * jnp.sin(lam_im[None] * steps)
    cp_re = c_re[None] * pw_re[:t, :, None, :] - c_im[None] * pw_im[:t, :, None, :]
    cp_im = c_re[None] * pw_im[:t, :, None, :] + c_im[None] * pw_re[:t, :, None, :]
    taps = (jnp.einsum("tgjp,gpk->tgjk", cp_re, bb_re, precision=hp)
            - jnp.einsum("tgjp,gpk->tgjk", cp_im, bb_im, precision=hp))
    lag = jnp.arange(t)[None, :] - jnp.arange(t)[:, None]
    toe = jnp.where((lag >= 0)[:, :, None, None, None], taps[jnp.clip(lag, 0, t - 1)], 0.0)
    toe = toe.transpose(2, 0, 4, 1, 3).reshape(-1, t * S5_GROUP, t * S5_GROUP)
    rev_re, rev_im = pw_re[t - 1::-1][:t], pw_im[t - 1::-1][:t]
    w_re = rev_re[..., None] * bb_re[None] - rev_im[..., None] * bb_im[None]
    w_im = rev_re[..., None] * bb_im[None] + rev_im[..., None] * bb_re[None]
    to_rows = lambda m: m.transpose(1, 0, 3, 2).reshape(m.shape[1], t * S5_GROUP, S5_STATE)
    w_re, w_im = to_rows(w_re), to_rows(w_im)
    mw = jnp.concatenate([w_re, w_im, w_im, w_re], axis=-1)
    cq_re = c_re[None] * pw_re[1:, :, None, :] - c_im[None] * pw_im[1:, :, None, :]
    cq_im = c_re[None] * pw_im[1:, :, None, :] + c_im[None] * pw_re[1:, :, None, :]
    to_cols = lambda m: m.transpose(1, 3, 0, 2).reshape(m.shape[1], S5_STATE, t * S5_GROUP)
    xm = jnp.concatenate([to_cols(cq_re), -to_cols(cq_im)], axis=1)
    ar, ai = pw_re[t], pw_im[t]
    a1 = jnp.concatenate([ar, ar, ar, ar], axis=-1)
    a2 = jnp.concatenate([-ai, ai, ai, -ai], axis=-1)
    rep = lambda m: jnp.broadcast_to(m.reshape(-1, S5_GROUP_CHUNK * 4 * S5_STATE)[:, None, :],
                                     (m.shape[0] // S5_GROUP_CHUNK, 2 * nb, S5_GROUP_CHUNK * 4 * S5_STATE))
    return mw.astype(BF16), toe.astype(BF16), xm.astype(BF16), rep(a1), rep(a2)


def _s5_mixer(h, g_mix, lam_re, lam_im, log_dt, b_re, b_im, c_re, c_im):
    b, lp, d = h.shape
    t, kk = S5_BLOCK, S5_GROUP
    groups = d // kk
    nblk = lp // t
    tm = ROW_TILE
    row_spec = pl.BlockSpec((None, tm, d), lambda i, j: (i, j, 0))
    a = pl.pallas_call(
        _norm_kernel,
        grid=(b, lp // tm),
        in_specs=[row_spec, _resident((1, d))],
        out_specs=row_spec,
        out_shape=jax.ShapeDtypeStruct((b, lp, d), BF16),
        compiler_params=_cparams("parallel", "parallel"),
    )(h, g_mix[None])
    u = a.reshape(b, nblk, t, groups, kk).transpose(3, 1, 0, 2, 4).reshape(groups, nblk * b, t * kk)
    mw, toe, xm, a1, a2 = _s5_operators(lam_re, lam_im, log_dt, b_re, b_im, c_re, c_im, b)
    gc = S5_GROUP_CHUNK
    rows, width = nblk * b, 4 * S5_STATE
    grp_spec = lambda r, c: pl.BlockSpec((gc, r, c), lambda i: (i, 0, 0))
    coef_spec = pl.BlockSpec((None, 2 * b, gc * width), lambda i: (i, 0, 0))
    y = pl.pallas_call(
        _s5_kernel,
        grid=(groups // gc,),
        in_specs=[grp_spec(rows, t * kk), grp_spec(t * kk, width), grp_spec(t * kk, t * kk),
                  grp_spec(2 * S5_STATE, t * kk), coef_spec, coef_spec],
        out_specs=grp_spec(rows, t * kk),
        out_shape=jax.ShapeDtypeStruct((groups, rows, t * kk), BF16),
        scratch_shapes=[pltpu.VMEM((rows, gc * width), F32), pltpu.VMEM((rows, gc * width), F32)],
        compiler_params=_cparams("parallel"),
    )(u, mw, toe, xm, a1, a2)
    return y.reshape(groups, nblk, b, t, kk).transpose(2, 1, 3, 0, 4).reshape(b, lp, d)


def _ret_kernel(h_ref, gmix_ref, win_ref, cos_ref, sin_ref, gn_ref, o_ref, state_ref, q_s, k_s, v_s, g_s):
    t = pl.program_id(1)
    tm, d = h_ref.shape
    nh = RET_HEADS
    dk = d // nh
    dv = 2 * dk
    c = RET_CHUNK
    half = dk // 2

    @pl.when(t == 0)
    def _():
        state_ref[...] = jnp.zeros_like(state_ref)

    a = _rms(h_ref[...], gmix_ref[...]).astype(BF16)
    cos, sin = cos_ref[...], sin_ref[...]
    for hd in range(nh):
        for ref, col0, scl in ((q_s, 0, 1.0), (k_s, d, dk ** -0.5)):
            x1 = _dot(a, win_ref[:, col0 + hd * dk:col0 + hd * dk + half])
            x2 = _dot(a, win_ref[:, col0 + hd * dk + half:col0 + (hd + 1) * dk])
            ref[:, hd * dk:hd * dk + half] = (x1 * cos - x2 * sin) * scl
            ref[:, hd * dk + half:(hd + 1) * dk] = (x1 * sin + x2 * cos) * scl
        v_s[:, hd * dv:(hd + 1) * dv] = _dot(a, win_ref[:, 2 * d + hd * dv:2 * d + (hd + 1) * dv]).astype(BF16)
        g_s[:, hd * dv:(hd + 1) * dv] = _silu(_dot(a, win_ref[:, 4 * d + hd * dv:4 * d + (hd + 1) * dv]))
    ri = lax.broadcasted_iota(jnp.int32, (c, c), 0)
    ci_ = lax.broadcasted_iota(jnp.int32, (c, c), 1)
    diff = (ri - ci_).astype(F32)
    pos = lax.broadcasted_iota(jnp.int32, (c, 1), 0).astype(F32)

    def chunk_body(cc, _):
        rows = pl.ds(pl.multiple_of(cc * c, c), c)
        for hd in range(nh):
            lg = math.log(1.0 - 2.0 ** (-5.0 - hd))
            q = q_s[rows, hd * dk:(hd + 1) * dk]
            k = k_s[rows, hd * dk:(hd + 1) * dk]
            v = v_s[rows, hd * dv:(hd + 1) * dv]
            st = state_ref[hd]
            decay = jnp.where(diff >= 0, jnp.exp(diff * lg), 0.0)
            scores = (_dot_nt(q.astype(BF16), k.astype(BF16)) * decay).astype(BF16)
            q_in = (q * jnp.exp((pos + 1.0) * lg)).astype(BF16)
            k_out = (k * jnp.exp((c - 1.0 - pos) * lg)).astype(BF16)
            o = _dot(scores, v) + _dot(q_in, st.astype(BF16))
            state_ref[hd] = st * math.exp(c * lg) + _dot_tn(k_out, v)
            mu = jnp.mean(o, axis=-1, keepdims=True)
            oc = o - mu
            var = jnp.mean(oc * oc, axis=-1, keepdims=True)
            o = oc * lax.rsqrt(var + EPS) * gn_ref[:, hd * dv:(hd + 1) * dv] * g_s[rows, hd * dv:(hd + 1) * dv]
            o_ref[rows, hd * dv:(hd + 1) * dv] = o.astype(o_ref.dtype)
        return 0

    lax.fori_loop(0, tm // c, chunk_body, 0)


def _ret_mixer(h, g_mix, w_in, gn_g):
    b, lp, d = h.shape
    nh = RET_HEADS
    dk = d // nh
    tm = ROW_TILE
    pos = jnp.maximum(jnp.arange(lp, dtype=F32) - PAD, 0.0)
    inv_freq = 1.0 / (ROPE_BASE ** (jnp.arange(0, dk, 2, dtype=F32) / dk))
    ang = pos[:, None] * inv_freq[None, :]
    cos, sin = jnp.cos(ang), jnp.sin(ang)
    row_spec = lambda width: pl.BlockSpec((None, tm, width), lambda i, t: (i, t, 0))
    tab_spec = pl.BlockSpec((tm, dk // 2), lambda i, t: (t, 0))
    win = w_in.astype(BF16)
    return pl.pallas_call(
        _ret_kernel,
        grid=(b, lp // tm),
        in_specs=[row_spec(d), _resident((1, d)), _resident(win.shape), tab_spec, tab_spec,
                  _resident((1, 2 * d))],
        out_specs=row_spec(2 * d),
        out_shape=jax.ShapeDtypeStruct((b, lp, 2 * d), BF16),
        scratch_shapes=[pltpu.VMEM((nh, dk, 2 * dk), F32), pltpu.VMEM((tm, d), F32), pltpu.VMEM((tm, d), F32),
                        pltpu.VMEM((tm, 2 * d), BF16), pltpu.VMEM((tm, 2 * d), F32)],
        compiler_params=_cparams("arbitrary", "arbitrary"),
    )(h, g_mix[None], win, cos, sin, gn_g[None])


def kernel(x, meta_tokens, norm_mix_g, norm_ffn_g, mla_w_down, mla_cq_norm_g, mla_ckv_norm_g, mla_w_uq, mla_w_ukv, mla_q_head_g, mla_k_head_g, mla_w_o, hgrn_w_in, hgrn_lb_logits, hgrn_o_norm_g, hgrn_w_o, s5_lam_re, s5_lam_im, s5_log_dt, s5_b_re, s5_b_im, s5_c_re, s5_c_im, s5_d, s5_w_glu, ret_w_in, ret_gn_g, ret_w_o, ffn_w_up, ffn_conv_w, ffn_conv_b, ffn_w_down):
    b, seq, d = x.shape
    depth = norm_mix_g.shape[0]
    h = jnp.concatenate([jnp.zeros((b, PAD, d), x.dtype),
                         jnp.broadcast_to(meta_tokens[None].astype(x.dtype), (b, N_META, d)), x], axis=1)
    lb_cum = jnp.cumsum(jax.nn.softmax(hgrn_lb_logits.astype(F32), axis=0), axis=0)
    lb_all = lb_cum - lb_cum[0:1]
    for i in range(depth):
        m, j = i % 4, i // 4
        ffn_w = (norm_ffn_g[i][None], ffn_w_up[i].astype(BF16), ffn_conv_w[i], ffn_conv_b[i][None],
                 ffn_w_down[i].astype(BF16))
        g_mix = norm_mix_g[i]
        if m == 0:
            o = _mla_mixer(h, g_mix, mla_w_down[j], mla_cq_norm_g[j], mla_ckv_norm_g[j], mla_w_uq[j],
                           mla_w_ukv[j], mla_q_head_g[j], mla_k_head_g[j])
            h = _ffn_after_proj(h, o, mla_w_o[j].astype(BF16), ffn_w)
        elif m == 1:
            o = _hgrn_mixer(h, g_mix, hgrn_w_in[j], lb_all[i], hgrn_o_norm_g[j])
            h = _ffn_after_proj(h, o, hgrn_w_o[j].astype(BF16), ffn_w)
        elif m == 2:
            y = _s5_mixer(h, g_mix, s5_lam_re[j], s5_lam_im[j], s5_log_dt[j], s5_b_re[j], s5_b_im[j],
                          s5_c_re[j], s5_c_im[j])
            h = _ffn_after_s5(h, y, g_mix[None], s5_d[j][None], s5_w_glu[j].astype(BF16), ffn_w)
        else:
            o = _ret_mixer(h, g_mix, ret_w_in[j], ret_gn_g[j])
            h = _ffn_after_proj(h, o, ret_w_o[j].astype(BF16), ffn_w)
    return h[:, FRONT:]
```

```python
import functools
import math

import jax
import jax.numpy as jnp
from jax import lax
from jax.experimental import pallas as pl
from jax.experimental.pallas import tpu as pltpu

F32 = jnp.float32
BF16 = jnp.bfloat16

N_META = 16
EPS = 1e-6
NEG_INF = -1e30
ROPE_BASE = 10000.0
MLA_HEADS = 8
MLA_NOPE = 128
MLA_ROPE = 64
MLA_V = 128
MLA_QK = MLA_NOPE + MLA_ROPE
MLA_Q_LORA = 384
MLA_KV_LORA = 256
HGRN_HEADS = 8
HGRN_DK = 128
S5_GROUP = 16
S5_STATE = 64
RET_HEADS = 4

LANES = 128
PAD = LANES - N_META
FRONT = PAD + N_META
ROW_TILE = 384
ATT_BLOCK = 256
ATT_SUBS = 4
HGRN_CHUNK = 64
HGRN_REF = HGRN_CHUNK // 2 - 1
RET_CHUNK = 128
S5_BLOCK = 16
S5_GROUP_CHUNK = 4
FFN_COLS = 256
VMEM_LIMIT = 56 * 1024 * 1024


def _cparams(*sem):
    return pltpu.CompilerParams(dimension_semantics=sem, vmem_limit_bytes=VMEM_LIMIT)


def _resident(shape):
    nd = len(shape)
    return pl.BlockSpec(shape, lambda *_: (0,) * nd, pipeline_mode=pl.Buffered(1))


def _rms(x, g):
    return x * lax.rsqrt(jnp.mean(x * x, axis=-1, keepdims=True) + EPS) * g


def _dot(a, b):
    return jnp.dot(a, b, preferred_element_type=F32)


def _dot_nt(a, b):
    return lax.dot_general(a, b, (((1,), (1,)), ((), ())), preferred_element_type=F32)


def _dot_tn(a, b):
    return lax.dot_general(a, b, (((0,), (0,)), ((), ())), preferred_element_type=F32)


def _sigmoid(x):
    return 1.0 / (1.0 + jnp.exp(-x))


def _silu(x):
    return x * _sigmoid(x)


def _gelu_tanh(x):
    return 0.5 * x * (1.0 + jnp.tanh(math.sqrt(2.0 / math.pi) * (x + 0.044715 * (x * x * x))))


def _tile_valid(t, tm):
    return t * tm + lax.broadcasted_iota(jnp.int32, (tm, 1), 0) >= PAD


def _ffn_tail(x, t, g_ref, wup_ref, cw_ref, cb_ref, wdn_ref, o_ref, carry_ref, act_ref):
    tm = x.shape[0]
    hidden = wdn_ref.shape[0]
    valid = _tile_valid(t, tm)
    x = jnp.where(valid, x, 0.0)
    a = _rms(x, g_ref[...]).astype(BF16)
    rid = lax.broadcasted_iota(jnp.int32, (tm, 1), 0)

    @pl.when(t == 0)
    def _():
        carry_ref[...] = jnp.zeros_like(carry_ref)

    def conv_slice(col0):
        cols = slice(col0, col0 + FFN_COLS)
        u = _dot(a, wup_ref[:, cols])
        prev = carry_ref[:, cols]
        p0, p1 = prev[6:7], prev[7:8]
        u1 = jnp.where(rid == 0, p1, pltpu.roll(u, 1, 0))
        u2 = jnp.where(rid == 0, p0, jnp.where(rid == 1, p1, pltpu.roll(u, 2, 0)))
        carry_ref[:, cols] = u[tm - 8:tm]
        w = cw_ref[:, cols]
        return w[0:1] * u2 + w[1:2] * u1 + w[2:3] * u + cb_ref[:, cols]

    for c in range(hidden // FFN_COLS):
        gate = conv_slice(c * FFN_COLS)
        val = conv_slice(hidden + c * FFN_COLS)
        act_ref[:, c * FFN_COLS:(c + 1) * FFN_COLS] = (_silu(gate) * val).astype(BF16)
    y = _dot(act_ref[...], wdn_ref[...])
    o_ref[...] = jnp.where(valid, x + y, 0.0)


def _ffn_proj_kernel(h_ref, o_in_ref, wo_ref, g_ref, wup_ref, cw_ref, cb_ref, wdn_ref,
                     o_ref, carry_ref, act_ref):
    t = pl.program_id(1)
    x = h_ref[...] + _dot(o_in_ref[...], wo_ref[...])
    _ffn_tail(x, t, g_ref, wup_ref, cw_ref, cb_ref, wdn_ref, o_ref, carry_ref, act_ref)


def _ffn_s5_kernel(h_ref, y_ref, gmix_ref, d_ref, wglu_ref, g_ref, wup_ref, cw_ref, cb_ref, wdn_ref,
                   o_ref, carry_ref, act_ref):
    t = pl.program_id(1)
    h = h_ref[...]
    d_model = h.shape[1]
    a = _rms(h, gmix_ref[...])
    z = _gelu_tanh(y_ref[...].astype(F32) + d_ref[...] * a).astype(BF16)
    val = _dot(z, wglu_ref[:, :d_model])
    gate = _dot(z, wglu_ref[:, d_model:])
    x = h + val * _sigmoid(gate)
    _ffn_tail(x, t, g_ref, wup_ref, cw_ref, cb_ref, wdn_ref, o_ref, carry_ref, act_ref)


def _ffn_call(kernel, h, lead_inputs, lead_specs, ffn_w):
    b, lp, d = h.shape
    g, wup, cw, cb, wdn = ffn_w
    hidden = wdn.shape[0]
    tm = ROW_TILE
    row_spec = lambda width: pl.BlockSpec((None, tm, width), lambda i, t: (i, t, 0))
    return pl.pallas_call(
        kernel,
        name=kernel.__name__.strip('_'),
        grid=(b, lp // tm),
        in_specs=[row_spec(d)] + lead_specs(row_spec) + [
            _resident(g.shape), _resident(wup.shape), _resident(cw.shape), _resident(cb.shape),
            _resident(wdn.shape)],
        out_specs=row_spec(d),
        out_shape=jax.ShapeDtypeStruct((b, lp, d), F32),
        scratch_shapes=[pltpu.VMEM((8, 2 * hidden), F32), pltpu.VMEM((tm, hidden), BF16)],
        compiler_params=_cparams("arbitrary", "arbitrary"),
    )(h, *lead_inputs, g, wup, cw, cb, wdn)


def _ffn_after_proj(h, o_in, wo, ffn_w):
    return _ffn_call(_ffn_proj_kernel, h, (o_in, wo),
                     lambda row_spec: [row_spec(o_in.shape[-1]), _resident(wo.shape)], ffn_w)


def _ffn_after_s5(h, y, gmix, dskip, wglu, ffn_w):
    return _ffn_call(_ffn_s5_kernel, h, (y, gmix, dskip, wglu),
                     lambda row_spec: [row_spec(y.shape[-1]), _resident(gmix.shape), _resident(dskip.shape),
                                       _resident(wglu.shape)], ffn_w)


def _rope_swap(r):
    lane = lax.broadcasted_iota(jnp.int32, r.shape, 1)
    half = MLA_ROPE // 2
    return jnp.where(lane < half, pltpu.roll(r, LANES - half, 1), pltpu.roll(r, half, 1))


def _mla_qkv_kernel(h_ref, gmix_ref, wd_ref, gcq_ref, gckv_ref, wuq_ref, wukv_ref,
                    gqn_ref, gqr_ref, gkn_ref, gkr_ref, cos_ref, sin_ref, q_ref, k_ref, v_ref):
    a = _rms(h_ref[...], gmix_ref[...]).astype(BF16)
    down = _dot(a, wd_ref[...])
    kv_lo = MLA_Q_LORA + MLA_KV_LORA
    cq = _rms(down[:, :MLA_Q_LORA], gcq_ref[...]).astype(BF16)
    ckv = _rms(down[:, MLA_Q_LORA:kv_lo], gckv_ref[...]).astype(BF16)
    kpe = down[:, kv_lo:kv_lo + LANES]
    q = _dot(cq, wuq_ref[...])
    kv = _dot(ckv, wukv_ref[...])
    cos, sin = cos_ref[...], sin_ref[...]

    def rope(r):
        return r * cos + _rope_swap(r) * sin

    ss_pe = jnp.sum(kpe * kpe, axis=-1, keepdims=True)
    k_rope = rope(kpe * gkr_ref[...])
    scale = MLA_QK ** -0.5 * math.log2(math.e)
    hw = 2 * LANES
    for hd in range(MLA_HEADS):
        qn = q[:, hd * hw:hd * hw + LANES]
        qr = q[:, hd * hw + LANES:(hd + 1) * hw]
        inv_q = lax.rsqrt(jnp.sum(qn * qn + qr * qr, axis=-1, keepdims=True) / MLA_QK + EPS) * scale
        q_ref[:, hd * hw:hd * hw + LANES] = (qn * inv_q * gqn_ref[...]).astype(BF16)
        q_ref[:, hd * hw + LANES:(hd + 1) * hw] = (rope(qr * gqr_ref[...]) * inv_q).astype(BF16)
        kn = kv[:, hd * hw:hd * hw + LANES]
        inv_k = lax.rsqrt((jnp.sum(kn * kn, axis=-1, keepdims=True) + ss_pe) / MLA_QK + EPS)
        k_ref[:, hd * hw:hd * hw + LANES] = (kn * inv_k * gkn_ref[...]).astype(BF16)
        k_ref[:, hd * hw + LANES:(hd + 1) * hw] = (k_rope * inv_k).astype(BF16)
        v_ref[:, hd * LANES:(hd + 1) * LANES] = kv[:, hd * hw + LANES:(hd + 1) * hw].astype(BF16)


def _attn_kernel(q_ref, k_ref, v_ref, o_ref):
    lp = q_ref.shape[0]
    blk, subs = ATT_BLOCK, ATT_SUBS
    n_sup = (lp - FRONT) // (blk * subs)
    k0, v0 = k_ref[0:FRONT], v_ref[0:FRONT]
    key_ok0 = lax.broadcasted_iota(jnp.int32, (1, FRONT), 1) >= PAD
    chunk = FRONT // 2
    diag_ok = (lax.broadcasted_iota(jnp.int32, (blk, blk), 0) // chunk
               >= lax.broadcasted_iota(jnp.int32, (blk, blk), 1) // chunk)

    def first_block(q):
        s = jnp.where(key_ok0, _dot_nt(q, k0), NEG_INF)
        m = jnp.max(s, axis=-1, keepdims=True)
        p = jnp.exp2(s - m)
        return m, jnp.sum(p, axis=-1, keepdims=True), _dot(p.astype(BF16), v0)

    def update(carry, s, vb):
        m, l, acc = carry
        m_new = jnp.maximum(m, jnp.max(s, axis=-1, keepdims=True))
        alpha = jnp.exp2(m - m_new)
        p = jnp.exp2(s - m_new)
        return (m_new, alpha * l + jnp.sum(p, axis=-1, keepdims=True),
                alpha * acc + _dot(p.astype(BF16), vb))

    _, l, acc = first_block(q_ref[0:FRONT])
    o_ref[0:FRONT] = (acc / l).astype(o_ref.dtype)

    def sup_body(i, _):
        base = FRONT + i * (blk * subs)
        rows = [pl.ds(pl.multiple_of(base + a * blk, LANES), blk) for a in range(subs)]

        def kv_body(j, carries):
            cols = pl.ds(pl.multiple_of(FRONT + j * blk, LANES), blk)
            kb, vb = k_ref[cols], v_ref[cols]
            return tuple(update(carries[a], _dot_nt(q_ref[rows[a]], kb), vb) for a in range(subs))

        carries = list(lax.fori_loop(0, i * subs, kv_body, tuple(first_block(q_ref[r]) for r in rows)))
        for c in range(subs):
            kb, vb = k_ref[rows[c]], v_ref[rows[c]]
            for a in range(c, subs):
                s = _dot_nt(q_ref[rows[a]], kb)
                carries[a] = update(carries[a], jnp.where(diag_ok, s, NEG_INF) if a == c else s, vb)
        for a in range(subs):
            _, l, acc = carries[a]
            o_ref[rows[a]] = (acc / l).astype(o_ref.dtype)
        return 0

    lax.fori_loop(0, n_sup, sup_body, 0)


def _mla_rope_tables(lp):
    half = MLA_ROPE // 2
    pos = jnp.maximum(jnp.arange(lp, dtype=F32) - PAD, 0.0)
    inv_freq = 1.0 / (ROPE_BASE ** (jnp.arange(0, MLA_ROPE, 2, dtype=F32) / MLA_ROPE))
    ang = pos[:, None] * inv_freq[None, :]
    c, s = jnp.cos(ang), jnp.sin(ang)
    z = jnp.zeros((lp, LANES - MLA_ROPE), F32)
    return jnp.concatenate([c, c, z], axis=1), jnp.concatenate([-s, s, z], axis=1)


def _mla_mixer(h, g_mix, w_down, g_cq, g_ckv, w_uq, w_ukv, g_qhead, g_khead):
    b, lp, d = h.shape
    nh, hw = MLA_HEADS, 2 * LANES
    kv_lo = MLA_Q_LORA + MLA_KV_LORA
    wd = jnp.pad(w_down, ((0, 0), (0, kv_lo + LANES - w_down.shape[1]))).astype(BF16)
    wuq = jnp.pad(w_uq.reshape(MLA_Q_LORA, nh, MLA_QK), ((0, 0), (0, 0), (0, hw - MLA_QK)))
    wuq = wuq.reshape(MLA_Q_LORA, nh * hw).astype(BF16)
    wukv = w_ukv.astype(BF16)
    zpad = jnp.zeros((LANES - MLA_ROPE,), F32)
    gqn, gkn = g_qhead[None, :MLA_NOPE], g_khead[None, :MLA_NOPE]
    gqr = jnp.concatenate([g_qhead[MLA_NOPE:], zpad])[None]
    gkr = jnp.concatenate([g_khead[MLA_NOPE:], zpad])[None]
    cos, sin = _mla_rope_tables(lp)
    tm = ROW_TILE
    row_spec = lambda width: pl.BlockSpec((None, tm, width), lambda i, t: (i, t, 0))
    tab_spec = pl.BlockSpec((tm, LANES), lambda i, t: (t, 0))
    small = [g_mix[None], wd, g_cq[None], g_ckv[None], wuq, wukv, gqn, gqr, gkn, gkr]
    q, k, v = pl.pallas_call(
        _mla_qkv_kernel,
        name='mla_qkv',
        grid=(b, lp // tm),
        in_specs=[row_spec(d)] + [_resident(w.shape) for w in small] + [tab_spec, tab_spec],
        out_specs=[row_spec(nh * hw), row_spec(nh * hw), row_spec(nh * MLA_V)],
        out_shape=[jax.ShapeDtypeStruct((b, lp, nh * hw), BF16), jax.ShapeDtypeStruct((b, lp, nh * hw), BF16),
                   jax.ShapeDtypeStruct((b, lp, nh * MLA_V), BF16)],
        compiler_params=_cparams("parallel", "parallel"),
    )(h, *small, cos, sin)
    head_spec = lambda width: pl.BlockSpec((None, lp, width), lambda i, j: (i, 0, j))
    return pl.pallas_call(
        _attn_kernel,
        name='mla_attention',
        grid=(b, nh),
        in_specs=[head_spec(hw), head_spec(hw), head_spec(MLA_V)],
        out_specs=head_spec(MLA_V),
        out_shape=jax.ShapeDtypeStruct((b, lp, nh * MLA_V), BF16),
        compiler_params=_cparams("parallel", "parallel"),
    )(q, k, v)


def _hgrn_kernel(h_ref, gmix_ref, win_ref, lb_ref, go_ref, tri_ref, o_ref,
                 state_ref, q_s, k_s, v_s, lf_s, gate_s):
    t = pl.program_id(1)
    tm, d = h_ref.shape
    dk = HGRN_DK
    c = HGRN_CHUNK

    @pl.when(t == 0)
    def _():
        state_ref[...] = jnp.zeros_like(state_ref)

    a = _rms(h_ref[...], gmix_ref[...]).astype(BF16)
    lb = lb_ref[...]
    q_s[...] = _silu(_dot(a, win_ref[:, 0:d]))
    forget = lb + (1.0 - lb) * _sigmoid(_dot(a, win_ref[:, d:2 * d]))
    lf_s[...] = jnp.log(forget)
    k_s[...] = 1.0 - forget
    v_s[...] = _dot(a, win_ref[:, 2 * d:3 * d]).astype(BF16)
    gate_s[...] = _silu(_dot(a, win_ref[:, 3 * d:4 * d]))
    causal = (lax.broadcasted_iota(jnp.int32, (c, c), 0) >= lax.broadcasted_iota(jnp.int32, (c, c), 1))
    tri = tri_ref[...]

    def chunk_body(ci, _):
        rows = pl.ds(pl.multiple_of(ci * c, c), c)
        lf = lf_s[rows]
        lf_hi = lf.astype(BF16)
        lf_lo = (lf - lf_hi.astype(F32)).astype(BF16)
        cum = _dot(tri, lf_hi) + _dot(tri, lf_lo)
        ref_row = cum[HGRN_REF:HGRN_REF + 1]
        last = cum[c - 1:c]
        q, k, v = q_s[rows], k_s[rows], v_s[rows]
        q_rel = (q * jnp.exp(cum - ref_row)).astype(BF16)
        k_rel = (k * jnp.exp(ref_row - cum)).astype(BF16)
        q_dec = (q * jnp.exp(cum)).astype(BF16)
        k_tail = (k * jnp.exp(last - cum)).astype(BF16)
        decay = jnp.exp(last)
        gate = gate_s[rows]
        for hd in range(d // dk):
            sl = slice(hd * dk, (hd + 1) * dk)
            st = state_ref[hd]
            attn = jnp.where(causal, _dot_nt(q_rel[:, sl], k_rel[:, sl]), 0.0).astype(BF16)
            o = _dot(attn, v[:, sl]) + _dot_nt(q_dec[:, sl], st.astype(BF16))
            state_ref[hd] = st * decay[:, sl] + _dot_tn(v[:, sl], k_tail[:, sl])
            o = _rms(o, go_ref[...]) * gate[:, sl]
            o_ref[rows, sl] = o.astype(o_ref.dtype)
        return 0

    lax.fori_loop(0, tm // c, chunk_body, 0)


def _hgrn_mixer(h, g_mix, w_in, lb, g_o):
    b, lp, d = h.shape
    tm = ROW_TILE
    c = HGRN_CHUNK
    tri = (jnp.arange(c)[:, None] >= jnp.arange(c)[None, :]).astype(BF16)
    row_spec = pl.BlockSpec((None, tm, d), lambda i, t: (i, t, 0))
    small = [g_mix[None], w_in.astype(BF16), lb[None], g_o[None], tri]
    return pl.pallas_call(
        _hgrn_kernel,
        name='hgrn2',
        grid=(b, lp // tm),
        in_specs=[row_spec] + [_resident(w.shape) for w in small],
        out_specs=row_spec,
        out_shape=jax.ShapeDtypeStruct((b, lp, d), BF16),
        scratch_shapes=[pltpu.VMEM((d // HGRN_DK, d // HGRN_HEADS, HGRN_DK), F32),
                        pltpu.VMEM((tm, d), F32), pltpu.VMEM((tm, d), F32), pltpu.VMEM((tm, d), BF16),
                        pltpu.VMEM((tm, d), F32), pltpu.VMEM((tm, d), F32)],
        compiler_params=_cparams("arbitrary", "arbitrary"),
    )(h, *small)


def _norm_kernel(h_ref, g_ref, o_ref):
    o_ref[...] = _rms(h_ref[...], g_ref[...]).astype(o_ref.dtype)


def _s5_kernel(u_ref, mw_ref, toe_ref, xm_ref, a1_ref, a2_ref, y_ref, w_s, x_s):
    gc, rows, width = u_ref.shape
    nb = a1_ref.shape[0] // 2
    for g in range(gc):
        w_s[:, g * width:(g + 1) * width] = _dot(u_ref[g], mw_ref[g])
    a1, a2 = a1_ref[...], a2_ref[...]
    top = lax.broadcasted_iota(jnp.int32, (2 * nb, 1), 0) < nb

    def swap_halves(z):
        parts = []
        for g in range(gc):
            parts += [z[:, g * width + width // 2:(g + 1) * width], z[:, g * width:g * width + width // 2]]
        return jnp.concatenate(parts, axis=1)

    def step(z, w):
        zr = pltpu.roll(z, nb, 0)
        return zr, a1 * zr + a2 * swap_halves(zr) + w

    def pair_body(v, z):
        rows8 = pl.ds(pl.multiple_of(v * 2 * nb, 2 * nb), 2 * nb)
        w = w_s[rows8]
        start_even, z = step(z, w)
        start_odd, z = step(z, w)
        x_s[rows8] = jnp.where(top, start_even, start_odd)
        return z

    lax.fori_loop(0, rows // (2 * nb), pair_body, jnp.zeros((2 * nb, gc * width), F32))
    half = width // 2
    for g in range(gc):
        x_start = x_s[:, g * width:g * width + half].astype(BF16)
        y_ref[g] = (_dot(u_ref[g], toe_ref[g]) + _dot(x_start, xm_ref[g])).astype(y_ref.dtype)


def _s5_operators(lam_re, lam_im, log_dt, b_re, b_im, c_re, c_im, nb):
    hp = lax.Precision.HIGHEST
    t = S5_BLOCK
    dt = jnp.exp(log_dt)[:, None]
    mag = jnp.exp(lam_re * dt)
    abar_re, abar_im = mag * jnp.cos(lam_im * dt), mag * jnp.sin(lam_im * dt)
    den = lam_re * lam_re + lam_im * lam_im
    zoh_re = ((abar_re - 1.0) * lam_re + abar_im * lam_im) / den
    zoh_im = (abar_im * lam_re - (abar_re - 1.0) * lam_im) / den
    bb_re = zoh_re[..., None] * b_re - zoh_im[..., None] * b_im
    bb_im = zoh_re[..., None] * b_im + zoh_im[..., None] * b_re
    steps = jnp.arange(t + 1, dtype=F32)[:, None, None] * dt[None]
    pmag = jnp.exp(lam_re[None] * steps)
    pw_re, pw_im = pmag * jnp.cos(lam_im[None] * steps), pmag * jnp.sin(lam_im[None] * steps)
    cp_re = c_re[None] * pw_re[:t, :, None, :] - c_im[None] * pw_im[:t, :, None, :]
    cp_im = c_re[None] * pw_im[:t, :, None, :] + c_im[None] * pw_re[:t, :, None, :]
    taps = (jnp.einsum("tgjp,gpk->tgjk", cp_re, bb_re, precision=hp)
            - jnp.einsum("tgjp,gpk->tgjk", cp_im, bb_im, precision=hp))
    lag = jnp.arange(t)[None, :] - jnp.arange(t)[:, None]
    toe = jnp.where((lag >= 0)[:, :, None, None, None], taps[jnp.clip(lag, 0, t - 1)], 0.0)
    toe = toe.transpose(2, 0, 4, 1, 3).reshape(-1, t * S5_GROUP, t * S5_GROUP)
    rev_re, rev_im = pw_re[t - 1::-1][:t], pw_im[t - 1::-1][:t]
    w_re = rev_re[..., None] * bb_re[None] - rev_im[..., None] * bb_im[None]
    w_im = rev_re[..., None] * bb_im[None] + rev_im[..., None] * bb_re[None]
    to_rows = lambda m: m.transpose(1, 0, 3, 2).reshape(m.shape[1], t * S5_GROUP, S5_STATE)
    w_re, w_im = to_rows(w_re), to_rows(w_im)
    mw = jnp.concatenate([w_re, w_im, w_im, w_re], axis=-1)
    cq_re = c_re[None] * pw_re[1:, :, None, :] - c_im[None] * pw_im[1:, :, None, :]
    cq_im = c_re[None] * pw_im[1:, :, None, :] + c_im[None] * pw_re[1:, :, None, :]
    to_cols = lambda m: m.transpose(1, 3, 0, 2).reshape(m.shape[1], S5_STATE, t * S5_GROUP)
    xm = jnp.concatenate([to_cols(cq_re), -to_cols(cq_im)], axis=1)
    ar, ai = pw_re[t], pw_im[t]
    a1 = jnp.concatenate([ar, ar, ar, ar], axis=-1)
    a2 = jnp.concatenate([-ai, ai, ai, -ai], axis=-1)
    rep = lambda m: jnp.broadcast_to(m.reshape(-1, S5_GROUP_CHUNK * 4 * S5_STATE)[:, None, :],
                                     (m.shape[0] // S5_GROUP_CHUNK, 2 * nb, S5_GROUP_CHUNK * 4 * S5_STATE))
    return mw.astype(BF16), toe.astype(BF16), xm.astype(BF16), rep(a1), rep(a2)


def _s5_mixer(h, g_mix, lam_re, lam_im, log_dt, b_re, b_im, c_re, c_im):
    b, lp, d = h.shape
    t, kk = S5_BLOCK, S5_GROUP
    groups = d // kk
    nblk = lp // t
    tm = ROW_TILE
    row_spec = pl.BlockSpec((None, tm, d), lambda i, j: (i, j, 0))
    a = pl.pallas_call(
        _norm_kernel,
        name='s5_norm',
        grid=(b, lp // tm),
        in_specs=[row_spec, _resident((1, d))],
        out_specs=row_spec,
        out_shape=jax.ShapeDtypeStruct((b, lp, d), BF16),
        compiler_params=_cparams("parallel", "parallel"),
    )(h, g_mix[None])
    u = a.reshape(b, nblk, t, groups, kk).transpose(3, 1, 0, 2, 4).reshape(groups, nblk * b, t * kk)
    mw, toe, xm, a1, a2 = _s5_operators(lam_re, lam_im, log_dt, b_re, b_im, c_re, c_im, b)
    gc = S5_GROUP_CHUNK
    rows, width = nblk * b, 4 * S5_STATE
    grp_spec = lambda r, c: pl.BlockSpec((gc, r, c), lambda i: (i, 0, 0))
    coef_spec = pl.BlockSpec((None, 2 * b, gc * width), lambda i: (i, 0, 0))
    y = pl.pallas_call(
        _s5_kernel,
        name='s5_scan',
        grid=(groups // gc,),
        in_specs=[grp_spec(rows, t * kk), grp_spec(t * kk, width), grp_spec(t * kk, t * kk),
                  grp_spec(2 * S5_STATE, t * kk), coef_spec, coef_spec],
        out_specs=grp_spec(rows, t * kk),
        out_shape=jax.ShapeDtypeStruct((groups, rows, t * kk), BF16),
        scratch_shapes=[pltpu.VMEM((rows, gc * width), F32), pltpu.VMEM((rows, gc * width), F32)],
        compiler_params=_cparams("parallel"),
    )(u, mw, toe, xm, a1, a2)
    return y.reshape(groups, nblk, b, t, kk).transpose(2, 1, 3, 0, 4).reshape(b, lp, d)


def _ret_kernel(h_ref, gmix_ref, win_ref, cos_ref, sin_ref, gn_ref, o_ref, state_ref, q_s, k_s, v_s, g_s):
    t = pl.program_id(1)
    tm, d = h_ref.shape
    nh = RET_HEADS
    dk = d // nh
    dv = 2 * dk
    c = RET_CHUNK
    half = dk // 2

    @pl.when(t == 0)
    def _():
        state_ref[...] = jnp.zeros_like(state_ref)

    a = _rms(h_ref[...], gmix_ref[...]).astype(BF16)
    cos, sin = cos_ref[...], sin_ref[...]
    for hd in range(nh):
        for ref, col0, scl in ((q_s, 0, 1.0), (k_s, d, dk ** -0.5)):
            x1 = _dot(a, win_ref[:, col0 + hd * dk:col0 + hd * dk + half])
            x2 = _dot(a, win_ref[:, col0 + hd * dk + half:col0 + (hd + 1) * dk])
            ref[:, hd * dk:hd * dk + half] = (x1 * cos - x2 * sin) * scl
            ref[:, hd * dk + half:(hd + 1) * dk] = (x1 * sin + x2 * cos) * scl
        v_s[:, hd * dv:(hd + 1) * dv] = _dot(a, win_ref[:, 2 * d + hd * dv:2 * d + (hd + 1) * dv]).astype(BF16)
        g_s[:, hd * dv:(hd + 1) * dv] = _silu(_dot(a, win_ref[:, 4 * d + hd * dv:4 * d + (hd + 1) * dv]))
    ri = lax.broadcasted_iota(jnp.int32, (c, c), 0)
    ci_ = lax.broadcasted_iota(jnp.int32, (c, c), 1)
    diff = (ri - ci_).astype(F32)
    pos = lax.broadcasted_iota(jnp.int32, (c, 1), 0).astype(F32)

    def chunk_body(cc, _):
        rows = pl.ds(pl.multiple_of(cc * c, c), c)
        for hd in range(nh):
            lg = math.log(1.0 - 2.0 ** (-5.0 - hd))
            q = q_s[rows, hd * dk:(hd + 1) * dk]
            k = k_s[rows, hd * dk:(hd + 1) * dk]
            v = v_s[rows, hd * dv:(hd + 1) * dv]
            st = state_ref[hd]
            decay = jnp.where(diff >= 0, jnp.exp(diff * lg), 0.0)
            scores = (_dot_nt(q.astype(BF16), k.astype(BF16)) * decay).astype(BF16)
            q_in = (q * jnp.exp((pos + 1.0) * lg)).astype(BF16)
            k_out = (k * jnp.exp((c - 1.0 - pos) * lg)).astype(BF16)
            o = _dot(scores, v) + _dot(q_in, st.astype(BF16))
            state_ref[hd] = st * math.exp(c * lg) + _dot_tn(k_out, v)
            mu = jnp.mean(o, axis=-1, keepdims=True)
            oc = o - mu
            var = jnp.mean(oc * oc, axis=-1, keepdims=True)
            o = oc * lax.rsqrt(var + EPS) * gn_ref[:, hd * dv:(hd + 1) * dv] * g_s[rows, hd * dv:(hd + 1) * dv]
            o_ref[rows, hd * dv:(hd + 1) * dv] = o.astype(o_ref.dtype)
        return 0

    lax.fori_loop(0, tm // c, chunk_body, 0)


def _ret_mixer(h, g_mix, w_in, gn_g):
    b, lp, d = h.shape
    nh = RET_HEADS
    dk = d // nh
    tm = ROW_TILE
    pos = jnp.maximum(jnp.arange(lp, dtype=F32) - PAD, 0.0)
    inv_freq = 1.0 / (ROPE_BASE ** (jnp.arange(0, dk, 2, dtype=F32) / dk))
    ang = pos[:, None] * inv_freq[None, :]
    cos, sin = jnp.cos(ang), jnp.sin(ang)
    row_spec = lambda width: pl.BlockSpec((None, tm, width), lambda i, t: (i, t, 0))
    tab_spec = pl.BlockSpec((tm, dk // 2), lambda i, t: (t, 0))
    win = w_in.astype(BF16)
    return pl.pallas_call(
        _ret_kernel,
        name='retention',
        grid=(b, lp // tm),
        in_specs=[row_spec(d), _resident((1, d)), _resident(win.shape), tab_spec, tab_spec,
                  _resident((1, 2 * d))],
        out_specs=row_spec(2 * d),
        out_shape=jax.ShapeDtypeStruct((b, lp, 2 * d), BF16),
        scratch_shapes=[pltpu.VMEM((nh, dk, 2 * dk), F32), pltpu.VMEM((tm, d), F32), pltpu.VMEM((tm, d), F32),
                        pltpu.VMEM((tm, 2 * d), BF16), pltpu.VMEM((tm, 2 * d), F32)],
        compiler_params=_cparams("arbitrary", "arbitrary"),
    )(h, g_mix[None], win, cos, sin, gn_g[None])


def kernel(x, meta_tokens, norm_mix_g, norm_ffn_g, mla_w_down, mla_cq_norm_g, mla_ckv_norm_g, mla_w_uq, mla_w_ukv, mla_q_head_g, mla_k_head_g, mla_w_o, hgrn_w_in, hgrn_lb_logits, hgrn_o_norm_g, hgrn_w_o, s5_lam_re, s5_lam_im, s5_log_dt, s5_b_re, s5_b_im, s5_c_re, s5_c_im, s5_d, s5_w_glu, ret_w_in, ret_gn_g, ret_w_o, ffn_w_up, ffn_conv_w, ffn_conv_b, ffn_w_down):
    b, seq, d = x.shape
    depth = norm_mix_g.shape[0]
    h = jnp.concatenate([jnp.zeros((b, PAD, d), x.dtype),
                         jnp.broadcast_to(meta_tokens[None].astype(x.dtype), (b, N_META, d)), x], axis=1)
    lb_cum = jnp.cumsum(jax.nn.softmax(hgrn_lb_logits.astype(F32), axis=0), axis=0)
    lb_all = lb_cum - lb_cum[0:1]
    for i in range(depth):
        m, j = i % 4, i // 4
        ffn_w = (norm_ffn_g[i][None], ffn_w_up[i].astype(BF16), ffn_conv_w[i], ffn_conv_b[i][None],
                 ffn_w_down[i].astype(BF16))
        g_mix = norm_mix_g[i]
        if m == 0:
            o = _mla_mixer(h, g_mix, mla_w_down[j], mla_cq_norm_g[j], mla_ckv_norm_g[j], mla_w_uq[j],
                           mla_w_ukv[j], mla_q_head_g[j], mla_k_head_g[j])
            h = _ffn_after_proj(h, o, mla_w_o[j].astype(BF16), ffn_w)
        elif m == 1:
            o = _hgrn_mixer(h, g_mix, hgrn_w_in[j], lb_all[i], hgrn_o_norm_g[j])
            h = _ffn_after_proj(h, o, hgrn_w_o[j].astype(BF16), ffn_w)
        elif m == 2:
            y = _s5_mixer(h, g_mix, s5_lam_re[j], s5_lam_im[j], s5_log_dt[j], s5_b_re[j], s5_b_im[j],
                          s5_c_re[j], s5_c_im[j])
            h = _ffn_after_s5(h, y, g_mix[None], s5_d[j][None], s5_w_glu[j].astype(BF16), ffn_w)
        else:
            o = _ret_mixer(h, g_mix, ret_w_in[j], ret_gn_g[j])
            h = _ffn_after_proj(h, o, ret_w_o[j].astype(BF16), ffn_w)
    return h[:, FRONT:]
```

```python
import functools
import math

import jax
import jax.numpy as jnp
from jax import lax
from jax.experimental import pallas as pl
from jax.experimental.pallas import tpu as pltpu

F32 = jnp.float32
BF16 = jnp.bfloat16

N_META = 16
EPS = 1e-6
NEG_INF = -1e30
ROPE_BASE = 10000.0
MLA_HEADS = 8
MLA_NOPE = 128
MLA_ROPE = 64
MLA_V = 128
MLA_QK = MLA_NOPE + MLA_ROPE
MLA_Q_LORA = 384
MLA_KV_LORA = 256
HGRN_HEADS = 8
HGRN_DK = 128
S5_GROUP = 16
S5_STATE = 64
RET_HEADS = 4

LANES = 128
PAD = LANES - N_META
FRONT = PAD + N_META
ROW_TILE = 384
ATT_BLOCK = 256
ATT_SUBS = 4
HGRN_CHUNK = 64
HGRN_REF = HGRN_CHUNK // 2 - 1
RET_CHUNK = 128
S5_BLOCK = 16
S5_SLAB_GROUPS = LANES // S5_GROUP
FFN_COLS = 256
VMEM_LIMIT = 56 * 1024 * 1024


def _cparams(*sem):
    return pltpu.CompilerParams(dimension_semantics=sem, vmem_limit_bytes=VMEM_LIMIT)


def _resident(shape):
    nd = len(shape)
    return pl.BlockSpec(shape, lambda *_: (0,) * nd, pipeline_mode=pl.Buffered(1))


def _rms(x, g):
    return x * lax.rsqrt(jnp.mean(x * x, axis=-1, keepdims=True) + EPS) * g


def _dot(a, b):
    return jnp.dot(a, b, preferred_element_type=F32)


def _dot_nt(a, b):
    return lax.dot_general(a, b, (((1,), (1,)), ((), ())), preferred_element_type=F32)


def _dot_tn(a, b):
    return lax.dot_general(a, b, (((0,), (0,)), ((), ())), preferred_element_type=F32)


def _sigmoid(x):
    return 1.0 / (1.0 + jnp.exp(-x))


def _silu(x):
    return x * _sigmoid(x)


def _gelu_tanh(x):
    return 0.5 * x * (1.0 + jnp.tanh(math.sqrt(2.0 / math.pi) * (x + 0.044715 * (x * x * x))))


def _tile_valid(t, tm):
    return t * tm + lax.broadcasted_iota(jnp.int32, (tm, 1), 0) >= PAD


def _ffn_tail(x, t, g_ref, wup_ref, cw_ref, cb_ref, wdn_ref, o_ref, carry_ref, act_ref):
    tm = x.shape[0]
    hidden = wdn_ref.shape[0]
    valid = _tile_valid(t, tm)
    x = jnp.where(valid, x, 0.0)
    a = _rms(x, g_ref[...]).astype(BF16)
    rid = lax.broadcasted_iota(jnp.int32, (tm, 1), 0)

    @pl.when(t == 0)
    def _():
        carry_ref[...] = jnp.zeros_like(carry_ref)

    def conv_slice(col0):
        cols = slice(col0, col0 + FFN_COLS)
        u = _dot(a, wup_ref[:, cols])
        prev = carry_ref[:, cols]
        p0, p1 = prev[6:7], prev[7:8]
        u1 = jnp.where(rid == 0, p1, pltpu.roll(u, 1, 0))
        u2 = jnp.where(rid == 0, p0, jnp.where(rid == 1, p1, pltpu.roll(u, 2, 0)))
        carry_ref[:, cols] = u[tm - 8:tm]
        w = cw_ref[:, cols]
        return w[0:1] * u2 + w[1:2] * u1 + w[2:3] * u + cb_ref[:, cols]

    for c in range(hidden // FFN_COLS):
        gate = conv_slice(c * FFN_COLS)
        val = conv_slice(hidden + c * FFN_COLS)
        act_ref[:, c * FFN_COLS:(c + 1) * FFN_COLS] = (_silu(gate) * val).astype(BF16)
    y = _dot(act_ref[...], wdn_ref[...])
    o_ref[...] = jnp.where(valid, x + y, 0.0)


def _ffn_proj_kernel(h_ref, o_in_ref, wo_ref, g_ref, wup_ref, cw_ref, cb_ref, wdn_ref,
                     o_ref, carry_ref, act_ref):
    t = pl.program_id(1)
    x = h_ref[...] + _dot(o_in_ref[...], wo_ref[...])
    _ffn_tail(x, t, g_ref, wup_ref, cw_ref, cb_ref, wdn_ref, o_ref, carry_ref, act_ref)


def _ffn_s5_kernel(h_ref, y_ref, gmix_ref, d_ref, wglu_ref, g_ref, wup_ref, cw_ref, cb_ref, wdn_ref,
                   o_ref, carry_ref, act_ref):
    t = pl.program_id(1)
    h = h_ref[...]
    d_model = h.shape[1]
    a = _rms(h, gmix_ref[...])
    z = _gelu_tanh(y_ref[...].astype(F32) + d_ref[...] * a).astype(BF16)
    val = _dot(z, wglu_ref[:, :d_model])
    gate = _dot(z, wglu_ref[:, d_model:])
    x = h + val * _sigmoid(gate)
    _ffn_tail(x, t, g_ref, wup_ref, cw_ref, cb_ref, wdn_ref, o_ref, carry_ref, act_ref)


def _ffn_call(kernel, h, lead_inputs, lead_specs, ffn_w):
    b, lp, d = h.shape
    g, wup, cw, cb, wdn = ffn_w
    hidden = wdn.shape[0]
    tm = ROW_TILE
    row_spec = lambda width: pl.BlockSpec((None, tm, width), lambda i, t: (i, t, 0))
    return pl.pallas_call(
        kernel,
        name=kernel.__name__.strip('_'),
        grid=(b, lp // tm),
        in_specs=[row_spec(d)] + lead_specs(row_spec) + [
            _resident(g.shape), _resident(wup.shape), _resident(cw.shape), _resident(cb.shape),
            _resident(wdn.shape)],
        out_specs=row_spec(d),
        out_shape=jax.ShapeDtypeStruct((b, lp, d), F32),
        scratch_shapes=[pltpu.VMEM((8, 2 * hidden), F32), pltpu.VMEM((tm, hidden), BF16)],
        compiler_params=_cparams("arbitrary", "arbitrary"),
    )(h, *lead_inputs, g, wup, cw, cb, wdn)


def _ffn_after_proj(h, o_in, wo, ffn_w):
    return _ffn_call(_ffn_proj_kernel, h, (o_in, wo),
                     lambda row_spec: [row_spec(o_in.shape[-1]), _resident(wo.shape)], ffn_w)


def _ffn_after_s5(h, y, gmix, dskip, wglu, ffn_w):
    return _ffn_call(_ffn_s5_kernel, h, (y, gmix, dskip, wglu),
                     lambda row_spec: [row_spec(y.shape[-1]), _resident(gmix.shape), _resident(dskip.shape),
                                       _resident(wglu.shape)], ffn_w)


def _rope_swap(r):
    lane = lax.broadcasted_iota(jnp.int32, r.shape, 1)
    half = MLA_ROPE // 2
    return jnp.where(lane < half, pltpu.roll(r, LANES - half, 1), pltpu.roll(r, half, 1))


def _mla_qkv_kernel(h_ref, gmix_ref, wd_ref, gcq_ref, gckv_ref, wuq_ref, wukv_ref,
                    gqn_ref, gqr_ref, gkn_ref, gkr_ref, cos_ref, sin_ref, q_ref, k_ref, v_ref):
    a = _rms(h_ref[...], gmix_ref[...]).astype(BF16)
    down = _dot(a, wd_ref[...])
    kv_lo = MLA_Q_LORA + MLA_KV_LORA
    cq = _rms(down[:, :MLA_Q_LORA], gcq_ref[...]).astype(BF16)
    ckv = _rms(down[:, MLA_Q_LORA:kv_lo], gckv_ref[...]).astype(BF16)
    kpe = down[:, kv_lo:kv_lo + LANES]
    q = _dot(cq, wuq_ref[...])
    kv = _dot(ckv, wukv_ref[...])
    cos, sin = cos_ref[...], sin_ref[...]

    def rope(r):
        return r * cos + _rope_swap(r) * sin

    ss_pe = jnp.sum(kpe * kpe, axis=-1, keepdims=True)
    k_rope = rope(kpe * gkr_ref[...])
    scale = MLA_QK ** -0.5 * math.log2(math.e)
    hw = 2 * LANES
    for hd in range(MLA_HEADS):
        qn = q[:, hd * hw:hd * hw + LANES]
        qr = q[:, hd * hw + LANES:(hd + 1) * hw]
        inv_q = lax.rsqrt(jnp.sum(qn * qn + qr * qr, axis=-1, keepdims=True) / MLA_QK + EPS) * scale
        q_ref[:, hd * hw:hd * hw + LANES] = (qn * inv_q * gqn_ref[...]).astype(BF16)
        q_ref[:, hd * hw + LANES:(hd + 1) * hw] = (rope(qr * gqr_ref[...]) * inv_q).astype(BF16)
        kn = kv[:, hd * hw:hd * hw + LANES]
        inv_k = lax.rsqrt((jnp.sum(kn * kn, axis=-1, keepdims=True) + ss_pe) / MLA_QK + EPS)
        k_ref[:, hd * hw:hd * hw + LANES] = (kn * inv_k * gkn_ref[...]).astype(BF16)
        k_ref[:, hd * hw + LANES:(hd + 1) * hw] = (k_rope * inv_k).astype(BF16)
        v_ref[:, hd * LANES:(hd + 1) * LANES] = kv[:, hd * hw + LANES:(hd + 1) * hw].astype(BF16)


def _attn_kernel(q_ref, k_ref, v_ref, o_ref, ve_s):
    lp, dv = v_ref.shape
    blk, subs = ATT_BLOCK, ATT_SUBS
    n_sup = (lp - FRONT) // (blk * subs)
    ve_s[:, 0:dv] = v_ref[...]
    ve_s[:, dv:2 * dv] = jnp.ones((lp, dv), BF16)
    k0, v0 = k_ref[0:FRONT], ve_s[0:FRONT]
    key_ok0 = lax.broadcasted_iota(jnp.int32, (1, FRONT), 1) >= PAD
    chunk = FRONT // 2
    diag_ok = (lax.broadcasted_iota(jnp.int32, (blk, blk), 0) // chunk
               >= lax.broadcasted_iota(jnp.int32, (blk, blk), 1) // chunk)

    def first_block(q):
        s = jnp.where(key_ok0, _dot_nt(q, k0), NEG_INF)
        m = jnp.max(s, axis=-1, keepdims=True)
        return m, _dot(jnp.exp2(s - m).astype(BF16), v0)

    def update(carry, s, vb):
        m, acc = carry
        m_new = jnp.maximum(m, jnp.max(s, axis=-1, keepdims=True))
        p = jnp.exp2(s - m_new).astype(BF16)
        return m_new, jnp.exp2(m - m_new) * acc + _dot(p, vb)

    def finish(carry):
        acc = carry[1]
        return (acc[:, 0:dv] / acc[:, dv:2 * dv]).astype(o_ref.dtype)

    o_ref[0:FRONT] = finish(first_block(q_ref[0:FRONT]))

    def sup_body(i, _):
        base = FRONT + i * (blk * subs)
        rows = [pl.ds(pl.multiple_of(base + a * blk, LANES), blk) for a in range(subs)]

        def key_rows(j):
            return pl.ds(pl.multiple_of(FRONT + j * blk, LANES), blk)

        def scores(j):
            kb = k_ref[key_rows(j)]
            return tuple(_dot_nt(q_ref[rows[a]], kb) for a in range(subs))

        def kv_body(j, state):
            carries, s_cur = state
            s_next = scores(j + 1)
            vb = ve_s[key_rows(j)]
            return tuple(update(carries[a], s_cur[a], vb) for a in range(subs)), s_next

        carries, s_cur = lax.fori_loop(0, i * subs, kv_body,
                                       (tuple(first_block(q_ref[r]) for r in rows), scores(0)))
        carries = list(carries)
        for c in range(subs):
            if c > 0:
                kb = k_ref[rows[c]]
                s_cur = [None] * c + [_dot_nt(q_ref[rows[a]], kb) for a in range(c, subs)]
            vb = ve_s[rows[c]]
            for a in range(c, subs):
                s = jnp.where(diag_ok, s_cur[a], NEG_INF) if a == c else s_cur[a]
                carries[a] = update(carries[a], s, vb)
        for a in range(subs):
            o_ref[rows[a]] = finish(carries[a])
        return 0

    lax.fori_loop(0, n_sup, sup_body, 0)


def _mla_rope_tables(lp):
    half = MLA_ROPE // 2
    pos = jnp.maximum(jnp.arange(lp, dtype=F32) - PAD, 0.0)
    inv_freq = 1.0 / (ROPE_BASE ** (jnp.arange(0, MLA_ROPE, 2, dtype=F32) / MLA_ROPE))
    ang = pos[:, None] * inv_freq[None, :]
    c, s = jnp.cos(ang), jnp.sin(ang)
    z = jnp.zeros((lp, LANES - MLA_ROPE), F32)
    return jnp.concatenate([c, c, z], axis=1), jnp.concatenate([-s, s, z], axis=1)


def _mla_mixer(h, g_mix, w_down, g_cq, g_ckv, w_uq, w_ukv, g_qhead, g_khead):
    b, lp, d = h.shape
    nh, hw = MLA_HEADS, 2 * LANES
    kv_lo = MLA_Q_LORA + MLA_KV_LORA
    wd = jnp.pad(w_down, ((0, 0), (0, kv_lo + LANES - w_down.shape[1]))).astype(BF16)
    wuq = jnp.pad(w_uq.reshape(MLA_Q_LORA, nh, MLA_QK), ((0, 0), (0, 0), (0, hw - MLA_QK)))
    wuq = wuq.reshape(MLA_Q_LORA, nh * hw).astype(BF16)
    wukv = w_ukv.astype(BF16)
    zpad = jnp.zeros((LANES - MLA_ROPE,), F32)
    gqn, gkn = g_qhead[None, :MLA_NOPE], g_khead[None, :MLA_NOPE]
    gqr = jnp.concatenate([g_qhead[MLA_NOPE:], zpad])[None]
    gkr = jnp.concatenate([g_khead[MLA_NOPE:], zpad])[None]
    cos, sin = _mla_rope_tables(lp)
    tm = ROW_TILE
    row_spec = lambda width: pl.BlockSpec((None, tm, width), lambda i, t: (i, t, 0))
    tab_spec = pl.BlockSpec((tm, LANES), lambda i, t: (t, 0))
    small = [g_mix[None], wd, g_cq[None], g_ckv[None], wuq, wukv, gqn, gqr, gkn, gkr]
    q, k, v = pl.pallas_call(
        _mla_qkv_kernel,
        name='mla_qkv',
        grid=(b, lp // tm),
        in_specs=[row_spec(d)] + [_resident(w.shape) for w in small] + [tab_spec, tab_spec],
        out_specs=[row_spec(nh * hw), row_spec(nh * hw), row_spec(nh * MLA_V)],
        out_shape=[jax.ShapeDtypeStruct((b, lp, nh * hw), BF16), jax.ShapeDtypeStruct((b, lp, nh * hw), BF16),
                   jax.ShapeDtypeStruct((b, lp, nh * MLA_V), BF16)],
        compiler_params=_cparams("parallel", "parallel"),
    )(h, *small, cos, sin)
    head_spec = lambda width: pl.BlockSpec((None, lp, width), lambda i, j: (i, 0, j))
    return pl.pallas_call(
        _attn_kernel,
        name='mla_attention',
        grid=(b, nh),
        in_specs=[head_spec(hw), head_spec(hw), head_spec(MLA_V)],
        out_specs=head_spec(MLA_V),
        out_shape=jax.ShapeDtypeStruct((b, lp, nh * MLA_V), BF16),
        scratch_shapes=[pltpu.VMEM((lp, 2 * MLA_V), BF16)],
        compiler_params=_cparams("parallel", "parallel"),
    )(q, k, v)


def _hgrn_kernel(h_ref, gmix_ref, win_ref, lb_ref, go_ref, tri_ref, o_ref,
                 state_ref, q_s, k_s, v_s, lf_s, gate_s):
    t = pl.program_id(1)
    tm, d = h_ref.shape
    dk = HGRN_DK
    c = HGRN_CHUNK

    @pl.when(t == 0)
    def _():
        state_ref[...] = jnp.zeros_like(state_ref)

    a = _rms(h_ref[...], gmix_ref[...]).astype(BF16)
    lb = lb_ref[...]
    q_s[...] = _silu(_dot(a, win_ref[:, 0:d]))
    forget = lb + (1.0 - lb) * _sigmoid(_dot(a, win_ref[:, d:2 * d]))
    lf_s[...] = jnp.log(forget)
    k_s[...] = 1.0 - forget
    v_s[...] = _dot(a, win_ref[:, 2 * d:3 * d]).astype(BF16)
    gate_s[...] = _silu(_dot(a, win_ref[:, 3 * d:4 * d]))
    causal = (lax.broadcasted_iota(jnp.int32, (c, c), 0) >= lax.broadcasted_iota(jnp.int32, (c, c), 1))
    tri = tri_ref[...]

    def chunk_body(ci, _):
        rows = pl.ds(pl.multiple_of(ci * c, c), c)
        lf = lf_s[rows]
        lf_hi = lf.astype(BF16)
        lf_lo = (lf - lf_hi.astype(F32)).astype(BF16)
        cum = _dot(tri, lf_hi) + _dot(tri, lf_lo)
        ref_row = cum[HGRN_REF:HGRN_REF + 1]
        last = cum[c - 1:c]
        q, k, v = q_s[rows], k_s[rows], v_s[rows]
        q_rel = (q * jnp.exp(cum - ref_row)).astype(BF16)
        k_rel = (k * jnp.exp(ref_row - cum)).astype(BF16)
        q_dec = (q * jnp.exp(cum)).astype(BF16)
        k_tail = (k * jnp.exp(last - cum)).astype(BF16)
        decay = jnp.exp(last)
        gate = gate_s[rows]
        for hd in range(d // dk):
            sl = slice(hd * dk, (hd + 1) * dk)
            st = state_ref[hd]
            attn = jnp.where(causal, _dot_nt(q_rel[:, sl], k_rel[:, sl]), 0.0).astype(BF16)
            o = _dot(attn, v[:, sl]) + _dot_nt(q_dec[:, sl], st.astype(BF16))
            state_ref[hd] = st * decay[:, sl] + _dot_tn(v[:, sl], k_tail[:, sl])
            o = _rms(o, go_ref[...]) * gate[:, sl]
            o_ref[rows, sl] = o.astype(o_ref.dtype)
        return 0

    lax.fori_loop(0, tm // c, chunk_body, 0)


def _hgrn_mixer(h, g_mix, w_in, lb, g_o):
    b, lp, d = h.shape
    tm = ROW_TILE
    c = HGRN_CHUNK
    tri = (jnp.arange(c)[:, None] >= jnp.arange(c)[None, :]).astype(BF16)
    row_spec = pl.BlockSpec((None, tm, d), lambda i, t: (i, t, 0))
    small = [g_mix[None], w_in.astype(BF16), lb[None], g_o[None], tri]
    return pl.pallas_call(
        _hgrn_kernel,
        name='hgrn2',
        grid=(b, lp // tm),
        in_specs=[row_spec] + [_resident(w.shape) for w in small],
        out_specs=row_spec,
        out_shape=jax.ShapeDtypeStruct((b, lp, d), BF16),
        scratch_shapes=[pltpu.VMEM((d // HGRN_DK, d // HGRN_HEADS, HGRN_DK), F32),
                        pltpu.VMEM((tm, d), F32), pltpu.VMEM((tm, d), F32), pltpu.VMEM((tm, d), BF16),
                        pltpu.VMEM((tm, d), F32), pltpu.VMEM((tm, d), F32)],
        compiler_params=_cparams("arbitrary", "arbitrary"),
    )(h, *small)


def _norm_kernel(h_ref, g_ref, o_ref):
    o_ref[...] = _rms(h_ref[...], g_ref[...]).astype(o_ref.dtype)


def _s5_kernel(a_ref, sel_ref, op_ref, xm_ref, a1_ref, a2_ref, y_ref, w_s, yt_s):
    lp = a_ref.shape[0]
    t = S5_BLOCK
    nblk = lp // t
    gc, _, width = sel_ref.shape
    half = width // 2
    steps = jnp.concatenate([a_ref[pl.ds(s, nblk, stride=t), :].astype(BF16) for s in range(t)], axis=1)
    for g in range(gc):
        u = _dot(steps, sel_ref[g]).astype(BF16)
        r = _dot(u, op_ref[g])
        w_s[:, g * width:(g + 1) * width] = r[:, :width]
        yt_s[:, g * width:(g + 1) * width] = r[:, width:]
    a1, a2 = a1_ref[...], a2_ref[...]

    def swap_halves(z):
        parts = []
        for g in range(gc):
            parts += [z[:, g * width + half:(g + 1) * width], z[:, g * width:g * width + half]]
        return jnp.concatenate(parts, axis=1)

    def step(blk, z):
        row = pl.ds(blk, 1)
        w = w_s[row, :]
        w_s[row, :] = z
        return a1 * z + a2 * swap_halves(z) + w

    lax.fori_loop(0, nblk, step, jnp.zeros((1, gc * width), F32))
    out = None
    for g in range(gc):
        x_start = w_s[:, g * width:g * width + half].astype(BF16)
        y = (yt_s[:, g * width:(g + 1) * width] + _dot(x_start, xm_ref[g])).astype(BF16)
        spread = _dot_nt(y, sel_ref[g])
        out = spread if out is None else out + spread
    for s in range(t):
        y_ref[pl.ds(s, nblk, stride=t), :] = out[:, s * LANES:(s + 1) * LANES]


def _s5_operators(lam_re, lam_im, log_dt, b_re, b_im, c_re, c_im):
    hp = lax.Precision.HIGHEST
    t = S5_BLOCK
    dt = jnp.exp(log_dt)[:, None]
    mag = jnp.exp(lam_re * dt)
    abar_re, abar_im = mag * jnp.cos(lam_im * dt), mag * jnp.sin(lam_im * dt)
    den = lam_re * lam_re + lam_im * lam_im
    zoh_re = ((abar_re - 1.0) * lam_re + abar_im * lam_im) / den
    zoh_im = (abar_im * lam_re - (abar_re - 1.0) * lam_im) / den
    bb_re = zoh_re[..., None] * b_re - zoh_im[..., None] * b_im
    bb_im = zoh_re[..., None] * b_im + zoh_im[..., None] * b_re
    steps = jnp.arange(t + 1, dtype=F32)[:, None, None] * dt[None]
    pmag = jnp.exp(lam_re[None] * steps)
    pw_re, pw_im = pmag * jnp.cos(lam_im[None] * steps), pmag * jnp.sin(lam_im[None] * steps)
    cp_re = c_re[None] * pw_re[:t, :, None, :] - c_im[None] * pw_im[:t, :, None, :]
    cp_im = c_re[None] * pw_im[:t, :, None, :] + c_im[None] * pw_re[:t, :, None, :]
    taps = (jnp.einsum("tgjp,gpk->tgjk", cp_re, bb_re, precision=hp)
            - jnp.einsum("tgjp,gpk->tgjk", cp_im, bb_im, precision=hp))
    lag = jnp.arange(t)[None, :] - jnp.arange(t)[:, None]
    toe = jnp.where((lag >= 0)[:, :, None, None, None], taps[jnp.clip(lag, 0, t - 1)], 0.0)
    toe = toe.transpose(2, 0, 4, 1, 3).reshape(-1, t * S5_GROUP, t * S5_GROUP)
    rev_re, rev_im = pw_re[t - 1::-1][:t], pw_im[t - 1::-1][:t]
    w_re = rev_re[..., None] * bb_re[None] - rev_im[..., None] * bb_im[None]
    w_im = rev_re[..., None] * bb_im[None] + rev_im[..., None] * bb_re[None]
    to_rows = lambda m: m.transpose(1, 0, 3, 2).reshape(m.shape[1], t * S5_GROUP, S5_STATE)
    w_re, w_im = to_rows(w_re), to_rows(w_im)
    mw = jnp.concatenate([w_re, w_im, w_im, w_re], axis=-1)
    cq_re = c_re[None] * pw_re[1:, :, None, :] - c_im[None] * pw_im[1:, :, None, :]
    cq_im = c_re[None] * pw_im[1:, :, None, :] + c_im[None] * pw_re[1:, :, None, :]
    to_cols = lambda m: m.transpose(1, 3, 0, 2).reshape(m.shape[1], S5_STATE, t * S5_GROUP)
    xm = jnp.concatenate([to_cols(cq_re), -to_cols(cq_im)], axis=1)
    ar, ai = pw_re[t], pw_im[t]
    a1 = jnp.concatenate([ar, ar, ar, ar], axis=-1)
    a2 = jnp.concatenate([-ai, ai, ai, -ai], axis=-1)
    per_slab = lambda m: m.reshape(-1, 1, S5_SLAB_GROUPS * 4 * S5_STATE)
    ops = jnp.concatenate([mw, toe], axis=-1).astype(BF16)
    return ops, xm.astype(BF16), per_slab(a1), per_slab(a2)


def _s5_selectors():
    t, kk = S5_BLOCK, S5_GROUP
    row = jnp.arange(t * LANES)
    col = jnp.arange(t * kk)
    hit = (row[:, None] // LANES == col[None, :] // kk) & (row[:, None] % kk == col[None, :] % kk)
    grp = (row % LANES) // kk
    return (hit[None] & (grp[None, :, None] == jnp.arange(S5_SLAB_GROUPS)[:, None, None])).astype(BF16)


def _s5_mixer(h, g_mix, lam_re, lam_im, log_dt, b_re, b_im, c_re, c_im):
    b, lp, d = h.shape
    t, kk = S5_BLOCK, S5_GROUP
    nblk = lp // t
    tm = ROW_TILE
    row_spec = pl.BlockSpec((None, tm, d), lambda i, j: (i, j, 0))
    a = pl.pallas_call(
        _norm_kernel,
        name='s5_norm',
        grid=(b, lp // tm),
        in_specs=[row_spec, _resident((1, d))],
        out_specs=row_spec,
        out_shape=jax.ShapeDtypeStruct((b, lp, d), F32),
        compiler_params=_cparams("parallel", "parallel"),
    )(h, g_mix[None])
    ops, xm, a1, a2 = _s5_operators(lam_re, lam_im, log_dt, b_re, b_im, c_re, c_im)
    sel = _s5_selectors()
    gc = S5_SLAB_GROUPS
    width = 4 * S5_STATE
    slab_spec = pl.BlockSpec((None, lp, LANES), lambda i, j: (j, 0, i))
    grp_spec = lambda r, c: pl.BlockSpec((gc, r, c), lambda i, j: (i, 0, 0))
    coef_spec = pl.BlockSpec((None, 1, gc * width), lambda i, j: (i, 0, 0))
    return pl.pallas_call(
        _s5_kernel,
        name='s5_scan',
        grid=(d // LANES, b),
        in_specs=[slab_spec, _resident(sel.shape), grp_spec(t * kk, width + t * kk),
                  grp_spec(2 * S5_STATE, t * kk), coef_spec, coef_spec],
        out_specs=slab_spec,
        out_shape=jax.ShapeDtypeStruct((b, lp, d), F32),
        scratch_shapes=[pltpu.VMEM((nblk, gc * width), F32), pltpu.VMEM((nblk, gc * width), F32)],
        compiler_params=_cparams("parallel", "parallel"),
    )(a, sel, ops, xm, a1, a2)


def _ret_kernel(h_ref, gmix_ref, win_ref, cos_ref, sin_ref, gn_ref, o_ref, state_ref, q_s, k_s, v_s, g_s):
    t = pl.program_id(1)
    tm, d = h_ref.shape
    nh = RET_HEADS
    dk = d // nh
    dv = 2 * dk
    c = RET_CHUNK
    half = dk // 2

    @pl.when(t == 0)
    def _():
        state_ref[...] = jnp.zeros_like(state_ref)

    a = _rms(h_ref[...], gmix_ref[...]).astype(BF16)
    cos, sin = cos_ref[...], sin_ref[...]
    for hd in range(nh):
        for ref, col0, scl in ((q_s, 0, 1.0), (k_s, d, dk ** -0.5)):
            x1 = _dot(a, win_ref[:, col0 + hd * dk:col0 + hd * dk + half])
            x2 = _dot(a, win_ref[:, col0 + hd * dk + half:col0 + (hd + 1) * dk])
            ref[:, hd * dk:hd * dk + half] = (x1 * cos - x2 * sin) * scl
            ref[:, hd * dk + half:(hd + 1) * dk] = (x1 * sin + x2 * cos) * scl
        v_s[:, hd * dv:(hd + 1) * dv] = _dot(a, win_ref[:, 2 * d + hd * dv:2 * d + (hd + 1) * dv]).astype(BF16)
        g_s[:, hd * dv:(hd + 1) * dv] = _silu(_dot(a, win_ref[:, 4 * d + hd * dv:4 * d + (hd + 1) * dv]))
    ri = lax.broadcasted_iota(jnp.int32, (c, c), 0)
    ci_ = lax.broadcasted_iota(jnp.int32, (c, c), 1)
    diff = (ri - ci_).astype(F32)
    pos = lax.broadcasted_iota(jnp.int32, (c, 1), 0).astype(F32)

    def chunk_body(cc, _):
        rows = pl.ds(pl.multiple_of(cc * c, c), c)
        for hd in range(nh):
            lg = math.log(1.0 - 2.0 ** (-5.0 - hd))
            q = q_s[rows, hd * dk:(hd + 1) * dk]
            k = k_s[rows, hd * dk:(hd + 1) * dk]
            v = v_s[rows, hd * dv:(hd + 1) * dv]
            st = state_ref[hd]
            decay = jnp.where(diff >= 0, jnp.exp(diff * lg), 0.0)
            scores = (_dot_nt(q.astype(BF16), k.astype(BF16)) * decay).astype(BF16)
            q_in = (q * jnp.exp((pos + 1.0) * lg)).astype(BF16)
            k_out = (k * jnp.exp((c - 1.0 - pos) * lg)).astype(BF16)
            o = _dot(scores, v) + _dot(q_in, st.astype(BF16))
            state_ref[hd] = st * math.exp(c * lg) + _dot_tn(k_out, v)
            mu = jnp.mean(o, axis=-1, keepdims=True)
            oc = o - mu
            var = jnp.mean(oc * oc, axis=-1, keepdims=True)
            o = oc * lax.rsqrt(var + EPS) * gn_ref[:, hd * dv:(hd + 1) * dv] * g_s[rows, hd * dv:(hd + 1) * dv]
            o_ref[rows, hd * dv:(hd + 1) * dv] = o.astype(o_ref.dtype)
        return 0

    lax.fori_loop(0, tm // c, chunk_body, 0)


def _ret_mixer(h, g_mix, w_in, gn_g):
    b, lp, d = h.shape
    nh = RET_HEADS
    dk = d // nh
    tm = ROW_TILE
    pos = jnp.maximum(jnp.arange(lp, dtype=F32) - PAD, 0.0)
    inv_freq = 1.0 / (ROPE_BASE ** (jnp.arange(0, dk, 2, dtype=F32) / dk))
    ang = pos[:, None] * inv_freq[None, :]
    cos, sin = jnp.cos(ang), jnp.sin(ang)
    row_spec = lambda width: pl.BlockSpec((None, tm, width), lambda i, t: (i, t, 0))
    tab_spec = pl.BlockSpec((tm, dk // 2), lambda i, t: (t, 0))
    win = w_in.astype(BF16)
    return pl.pallas_call(
        _ret_kernel,
        name='retention',
        grid=(b, lp // tm),
        in_specs=[row_spec(d), _resident((1, d)), _resident(win.shape), tab_spec, tab_spec,
                  _resident((1, 2 * d))],
        out_specs=row_spec(2 * d),
        out_shape=jax.ShapeDtypeStruct((b, lp, 2 * d), BF16),
        scratch_shapes=[pltpu.VMEM((nh, dk, 2 * dk), F32), pltpu.VMEM((tm, d), F32), pltpu.VMEM((tm, d), F32),
                        pltpu.VMEM((tm, 2 * d), BF16), pltpu.VMEM((tm, 2 * d), F32)],
        compiler_params=_cparams("arbitrary", "arbitrary"),
    )(h, g_mix[None], win, cos, sin, gn_g[None])


def kernel(x, meta_tokens, norm_mix_g, norm_ffn_g, mla_w_down, mla_cq_norm_g, mla_ckv_norm_g, mla_w_uq, mla_w_ukv, mla_q_head_g, mla_k_head_g, mla_w_o, hgrn_w_in, hgrn_lb_logits, hgrn_o_norm_g, hgrn_w_o, s5_lam_re, s5_lam_im, s5_log_dt, s5_b_re, s5_b_im, s5_c_re, s5_c_im, s5_d, s5_w_glu, ret_w_in, ret_gn_g, ret_w_o, ffn_w_up, ffn_conv_w, ffn_conv_b, ffn_w_down):
    b, seq, d = x.shape
    depth = norm_mix_g.shape[0]
    h = jnp.concatenate([jnp.zeros((b, PAD, d), x.dtype),
                         jnp.broadcast_to(meta_tokens[None].astype(x.dtype), (b, N_META, d)), x], axis=1)
    lb_cum = jnp.cumsum(jax.nn.softmax(hgrn_lb_logits.astype(F32), axis=0), axis=0)
    lb_all = lb_cum - lb_cum[0:1]
    for i in range(depth):
        m, j = i % 4, i // 4
        ffn_w = (norm_ffn_g[i][None], ffn_w_up[i].astype(BF16), ffn_conv_w[i], ffn_conv_b[i][None],
                 ffn_w_down[i].astype(BF16))
        g_mix = norm_mix_g[i]
        if m == 0:
            o = _mla_mixer(h, g_mix, mla_w_down[j], mla_cq_norm_g[j], mla_ckv_norm_g[j], mla_w_uq[j],
                           mla_w_ukv[j], mla_q_head_g[j], mla_k_head_g[j])
            h = _ffn_after_proj(h, o, mla_w_o[j].astype(BF16), ffn_w)
        elif m == 1:
            o = _hgrn_mixer(h, g_mix, hgrn_w_in[j], lb_all[i], hgrn_o_norm_g[j])
            h = _ffn_after_proj(h, o, hgrn_w_o[j].astype(BF16), ffn_w)
        elif m == 2:
            y = _s5_mixer(h, g_mix, s5_lam_re[j], s5_lam_im[j], s5_log_dt[j], s5_b_re[j], s5_b_im[j],
                          s5_c_re[j], s5_c_im[j])
            h = _ffn_after_s5(h, y, g_mix[None], s5_d[j][None], s5_w_glu[j].astype(BF16), ffn_w)
        else:
            o = _ret_mixer(h, g_mix, ret_w_in[j], ret_gn_g[j])
            h = _ffn_after_proj(h, o, ret_w_o[j].astype(BF16), ffn_w)
    return h[:, FRONT:]
```

```python
import functools
import math

import jax
import jax.numpy as jnp
from jax import lax
from jax.experimental import pallas as pl
from jax.experimental.pallas import tpu as pltpu

F32 = jnp.float32
BF16 = jnp.bfloat16

N_META = 16
EPS = 1e-6
NEG_INF = -1e30
ROPE_BASE = 10000.0
MLA_HEADS = 8
MLA_NOPE = 128
MLA_ROPE = 64
MLA_V = 128
MLA_QK = MLA_NOPE + MLA_ROPE
MLA_Q_LORA = 384
MLA_KV_LORA = 256
HGRN_HEADS = 8
HGRN_DK = 128
S5_GROUP = 16
S5_STATE = 64
RET_HEADS = 4

LANES = 128
PAD = LANES - N_META
FRONT = PAD + N_META
ROW_TILE = 384
FFN_ROW_TILE = 704
ATT_BLOCK = 256
ATT_SUBS = 4
HGRN_CHUNK = 64
HGRN_REF = HGRN_CHUNK // 2 - 1
RET_CHUNK = 128
S5_BLOCK = 16
S5_SLAB_GROUPS = LANES // S5_GROUP
FFN_COLS = 256
VMEM_LIMIT = 56 * 1024 * 1024


def _cparams(*sem):
    return pltpu.CompilerParams(dimension_semantics=sem, vmem_limit_bytes=VMEM_LIMIT)


def _resident(shape):
    nd = len(shape)
    return pl.BlockSpec(shape, lambda *_: (0,) * nd, pipeline_mode=pl.Buffered(1))


def _rms(x, g):
    return x * lax.rsqrt(jnp.mean(x * x, axis=-1, keepdims=True) + EPS) * g


def _dot(a, b):
    return jnp.dot(a, b, preferred_element_type=F32)


def _dot_nt(a, b):
    return lax.dot_general(a, b, (((1,), (1,)), ((), ())), preferred_element_type=F32)


def _dot_tn(a, b):
    return lax.dot_general(a, b, (((0,), (0,)), ((), ())), preferred_element_type=F32)


def _sigmoid(x):
    return 1.0 / (1.0 + jnp.exp(-x))


def _silu(x):
    return x * _sigmoid(x)


def _gelu_tanh(x):
    return 0.5 * x * (1.0 + jnp.tanh(math.sqrt(2.0 / math.pi) * (x + 0.044715 * (x * x * x))))


def _tile_valid(t, tm):
    return t * tm + lax.broadcasted_iota(jnp.int32, (tm, 1), 0) >= PAD


def _ffn_tail(x, t, g_ref, wup_ref, cw_ref, cb_ref, wdn_ref, o_ref, carry_ref, act_ref):
    tm = x.shape[0]
    hidden = wdn_ref.shape[0]
    valid = _tile_valid(t, tm)
    x = jnp.where(valid, x, 0.0)
    a = _rms(x, g_ref[...]).astype(BF16)
    rid = lax.broadcasted_iota(jnp.int32, (tm, 1), 0)

    @pl.when(t == 0)
    def _():
        carry_ref[...] = jnp.zeros_like(carry_ref)

    def conv_slice(col0):
        cols = slice(col0, col0 + FFN_COLS)
        u = _dot(a, wup_ref[:, cols])
        prev = carry_ref[:, cols]
        p0, p1 = prev[6:7], prev[7:8]
        u1 = jnp.where(rid == 0, p1, pltpu.roll(u, 1, 0))
        u2 = jnp.where(rid == 0, p0, jnp.where(rid == 1, p1, pltpu.roll(u, 2, 0)))
        carry_ref[:, cols] = u[tm - 8:tm]
        w = cw_ref[:, cols]
        return w[0:1] * u2 + w[1:2] * u1 + w[2:3] * u + cb_ref[:, cols]

    for c in range(hidden // FFN_COLS):
        gate = conv_slice(c * FFN_COLS)
        val = conv_slice(hidden + c * FFN_COLS)
        act_ref[:, c * FFN_COLS:(c + 1) * FFN_COLS] = (_silu(gate) * val).astype(BF16)
    y = _dot(act_ref[...], wdn_ref[...])
    o_ref[...] = jnp.where(valid, x + y, 0.0)


def _ffn_proj_kernel(h_ref, o_in_ref, wo_ref, g_ref, wup_ref, cw_ref, cb_ref, wdn_ref,
                     o_ref, carry_ref, act_ref):
    t = pl.program_id(1)
    x = h_ref[...] + _dot(o_in_ref[...], wo_ref[...])
    _ffn_tail(x, t, g_ref, wup_ref, cw_ref, cb_ref, wdn_ref, o_ref, carry_ref, act_ref)


def _ffn_s5_kernel(h_ref, y_ref, gmix_ref, d_ref, wglu_ref, g_ref, wup_ref, cw_ref, cb_ref, wdn_ref,
                   o_ref, carry_ref, act_ref):
    t = pl.program_id(1)
    h = h_ref[...]
    d_model = h.shape[1]
    a = _rms(h, gmix_ref[...])
    z = _gelu_tanh(y_ref[...].astype(F32) + d_ref[...] * a).astype(BF16)
    val = _dot(z, wglu_ref[:, :d_model])
    gate = _dot(z, wglu_ref[:, d_model:])
    x = h + val * _sigmoid(gate)
    _ffn_tail(x, t, g_ref, wup_ref, cw_ref, cb_ref, wdn_ref, o_ref, carry_ref, act_ref)


def _ffn_call(kernel, h, lead_inputs, lead_specs, ffn_w):
    b, lp, d = h.shape
    g, wup, cw, cb, wdn = ffn_w
    hidden = wdn.shape[0]
    tm = FFN_ROW_TILE if lp % FFN_ROW_TILE == 0 else ROW_TILE
    row_spec = lambda width: pl.BlockSpec((None, tm, width), lambda i, t: (i, t, 0))
    return pl.pallas_call(
        kernel,
        name=kernel.__name__.strip('_'),
        grid=(b, lp // tm),
        in_specs=[row_spec(d)] + lead_specs(row_spec) + [
            _resident(g.shape), _resident(wup.shape), _resident(cw.shape), _resident(cb.shape),
            _resident(wdn.shape)],
        out_specs=row_spec(d),
        out_shape=jax.ShapeDtypeStruct((b, lp, d), F32),
        scratch_shapes=[pltpu.VMEM((8, 2 * hidden), F32), pltpu.VMEM((tm, hidden), BF16)],
        compiler_params=_cparams("arbitrary", "arbitrary"),
    )(h, *lead_inputs, g, wup, cw, cb, wdn)


def _ffn_after_proj(h, o_in, wo, ffn_w):
    return _ffn_call(_ffn_proj_kernel, h, (o_in, wo),
                     lambda row_spec: [row_spec(o_in.shape[-1]), _resident(wo.shape)], ffn_w)


def _ffn_after_s5(h, y, gmix, dskip, wglu, ffn_w):
    return _ffn_call(_ffn_s5_kernel, h, (y, gmix, dskip, wglu),
                     lambda row_spec: [row_spec(y.shape[-1]), _resident(gmix.shape), _resident(dskip.shape),
                                       _resident(wglu.shape)], ffn_w)


def _rope_swap(r):
    lane = lax.broadcasted_iota(jnp.int32, r.shape, 1)
    half = MLA_ROPE // 2
    return jnp.where(lane < half, pltpu.roll(r, LANES - half, 1), pltpu.roll(r, half, 1))


def _mla_qkv_kernel(h_ref, gmix_ref, wd_ref, gcq_ref, gckv_ref, wuq_ref, wukv_ref,
                    gqn_ref, gqr_ref, gkn_ref, gkr_ref, cos_ref, sin_ref, q_ref, k_ref, v_ref):
    a = _rms(h_ref[...], gmix_ref[...]).astype(BF16)
    down = _dot(a, wd_ref[...])
    kv_lo = MLA_Q_LORA + MLA_KV_LORA
    cq = _rms(down[:, :MLA_Q_LORA], gcq_ref[...]).astype(BF16)
    ckv = _rms(down[:, MLA_Q_LORA:kv_lo], gckv_ref[...]).astype(BF16)
    kpe = down[:, kv_lo:kv_lo + LANES]
    q = _dot(cq, wuq_ref[...])
    kv = _dot(ckv, wukv_ref[...])
    cos, sin = cos_ref[...], sin_ref[...]

    def rope(r):
        return r * cos + _rope_swap(r) * sin

    ss_pe = jnp.sum(kpe * kpe, axis=-1, keepdims=True)
    k_rope = rope(kpe * gkr_ref[...])
    scale = MLA_QK ** -0.5 * math.log2(math.e)
    hw = 2 * LANES
    for hd in range(MLA_HEADS):
        qn = q[:, hd * hw:hd * hw + LANES]
        qr = q[:, hd * hw + LANES:(hd + 1) * hw]
        inv_q = lax.rsqrt(jnp.sum(qn * qn + qr * qr, axis=-1, keepdims=True) / MLA_QK + EPS) * scale
        q_ref[:, hd * hw:hd * hw + LANES] = (qn * inv_q * gqn_ref[...]).astype(BF16)
        q_ref[:, hd * hw + LANES:(hd + 1) * hw] = (rope(qr * gqr_ref[...]) * inv_q).astype(BF16)
        kn = kv[:, hd * hw:hd * hw + LANES]
        inv_k = lax.rsqrt((jnp.sum(kn * kn, axis=-1, keepdims=True) + ss_pe) / MLA_QK + EPS)
        k_ref[:, hd * hw:hd * hw + LANES] = (kn * inv_k * gkn_ref[...]).astype(BF16)
        k_ref[:, hd * hw + LANES:(hd + 1) * hw] = (k_rope * inv_k).astype(BF16)
        v_ref[:, hd * LANES:(hd + 1) * LANES] = kv[:, hd * hw + LANES:(hd + 1) * hw].astype(BF16)


def _attn_kernel(q_ref, k_ref, v_ref, o_ref, ve_s):
    lp, dv = v_ref.shape
    blk, subs = ATT_BLOCK, ATT_SUBS
    n_sup = (lp - FRONT) // (blk * subs)
    ve_s[:, 0:dv] = v_ref[...]
    ve_s[:, dv:2 * dv] = jnp.ones((lp, dv), BF16)
    k0, v0 = k_ref[0:FRONT], ve_s[0:FRONT]
    key_ok0 = lax.broadcasted_iota(jnp.int32, (1, FRONT), 1) >= PAD
    chunk = FRONT // 2
    diag_ok = (lax.broadcasted_iota(jnp.int32, (blk, blk), 0) // chunk
               >= lax.broadcasted_iota(jnp.int32, (blk, blk), 1) // chunk)

    def first_block(q):
        s = jnp.where(key_ok0, _dot_nt(q, k0), NEG_INF)
        m = jnp.max(s, axis=-1, keepdims=True)
        return m, _dot(jnp.exp2(s - m).astype(BF16), v0)

    def update(carry, s, vb):
        m, acc = carry
        m_new = jnp.maximum(m, jnp.max(s, axis=-1, keepdims=True))
        p = jnp.exp2(s - m_new).astype(BF16)
        return m_new, jnp.exp2(m - m_new) * acc + _dot(p, vb)

    def finish(carry):
        acc = carry[1]
        return (acc[:, 0:dv] / acc[:, dv:2 * dv]).astype(o_ref.dtype)

    o_ref[0:FRONT] = finish(first_block(q_ref[0:FRONT]))

    def sup_body(i, _):
        base = FRONT + i * (blk * subs)
        rows = [pl.ds(pl.multiple_of(base + a * blk, LANES), blk) for a in range(subs)]

        def key_rows(j):
            return pl.ds(pl.multiple_of(FRONT + j * blk, LANES), blk)

        def scores(j):
            kb = k_ref[key_rows(j)]
            return tuple(_dot_nt(q_ref[rows[a]], kb) for a in range(subs))

        def kv_body(j, state):
            carries, s_cur = state
            s_next = scores(j + 1)
            vb = ve_s[key_rows(j)]
            return tuple(update(carries[a], s_cur[a], vb) for a in range(subs)), s_next

        carries, s_cur = lax.fori_loop(0, i * subs, kv_body,
                                       (tuple(first_block(q_ref[r]) for r in rows), scores(0)))
        carries = list(carries)
        for c in range(subs):
            if c > 0:
                kb = k_ref[rows[c]]
                s_cur = [None] * c + [_dot_nt(q_ref[rows[a]], kb) for a in range(c, subs)]
            vb = ve_s[rows[c]]
            for a in range(c, subs):
                s = jnp.where(diag_ok, s_cur[a], NEG_INF) if a == c else s_cur[a]
                carries[a] = update(carries[a], s, vb)
        for a in range(subs):
            o_ref[rows[a]] = finish(carries[a])
        return 0

    lax.fori_loop(0, n_sup, sup_body, 0)


def _mla_rope_tables(lp):
    half = MLA_ROPE // 2
    pos = jnp.maximum(jnp.arange(lp, dtype=F32) - PAD, 0.0)
    inv_freq = 1.0 / (ROPE_BASE ** (jnp.arange(0, MLA_ROPE, 2, dtype=F32) / MLA_ROPE))
    ang = pos[:, None] * inv_freq[None, :]
    c, s = jnp.cos(ang), jnp.sin(ang)
    z = jnp.zeros((lp, LANES - MLA_ROPE), F32)
    return jnp.concatenate([c, c, z], axis=1), jnp.concatenate([-s, s, z], axis=1)


def _mla_mixer(h, g_mix, w_down, g_cq, g_ckv, w_uq, w_ukv, g_qhead, g_khead):
    b, lp, d = h.shape
    nh, hw = MLA_HEADS, 2 * LANES
    kv_lo = MLA_Q_LORA + MLA_KV_LORA
    wd = jnp.pad(w_down, ((0, 0), (0, kv_lo + LANES - w_down.shape[1]))).astype(BF16)
    wuq = jnp.pad(w_uq.reshape(MLA_Q_LORA, nh, MLA_QK), ((0, 0), (0, 0), (0, hw - MLA_QK)))
    wuq = wuq.reshape(MLA_Q_LORA, nh * hw).astype(BF16)
    wukv = w_ukv.astype(BF16)
    zpad = jnp.zeros((LANES - MLA_ROPE,), F32)
    gqn, gkn = g_qhead[None, :MLA_NOPE], g_khead[None, :MLA_NOPE]
    gqr = jnp.concatenate([g_qhead[MLA_NOPE:], zpad])[None]
    gkr = jnp.concatenate([g_khead[MLA_NOPE:], zpad])[None]
    cos, sin = _mla_rope_tables(lp)
    tm = ROW_TILE
    row_spec = lambda width: pl.BlockSpec((None, tm, width), lambda i, t: (i, t, 0))
    tab_spec = pl.BlockSpec((tm, LANES), lambda i, t: (t, 0))
    small = [g_mix[None], wd, g_cq[None], g_ckv[None], wuq, wukv, gqn, gqr, gkn, gkr]
    q, k, v = pl.pallas_call(
        _mla_qkv_kernel,
        name='mla_qkv',
        grid=(b, lp // tm),
        in_specs=[row_spec(d)] + [_resident(w.shape) for w in small] + [tab_spec, tab_spec],
        out_specs=[row_spec(nh * hw), row_spec(nh * hw), row_spec(nh * MLA_V)],
        out_shape=[jax.ShapeDtypeStruct((b, lp, nh * hw), BF16), jax.ShapeDtypeStruct((b, lp, nh * hw), BF16),
                   jax.ShapeDtypeStruct((b, lp, nh * MLA_V), BF16)],
        compiler_params=_cparams("parallel", "parallel"),
    )(h, *small, cos, sin)
    head_spec = lambda width: pl.BlockSpec((None, lp, width), lambda i, j: (i, 0, j))
    return pl.pallas_call(
        _attn_kernel,
        name='mla_attention',
        grid=(b, nh),
        in_specs=[head_spec(hw), head_spec(hw), head_spec(MLA_V)],
        out_specs=head_spec(MLA_V),
        out_shape=jax.ShapeDtypeStruct((b, lp, nh * MLA_V), BF16),
        scratch_shapes=[pltpu.VMEM((lp, 2 * MLA_V), BF16)],
        compiler_params=_cparams("parallel", "parallel"),
    )(q, k, v)


def _hgrn_kernel(h_ref, gmix_ref, win_ref, lb_ref, go_ref, tri_ref, o_ref,
                 state_ref, q_s, k_s, v_s, lf_s, gate_s):
    t = pl.program_id(1)
    tm, d = h_ref.shape
    dk = HGRN_DK
    c = HGRN_CHUNK

    @pl.when(t == 0)
    def _():
        state_ref[...] = jnp.zeros_like(state_ref)

    a = _rms(h_ref[...], gmix_ref[...]).astype(BF16)
    lb = lb_ref[...]
    q_s[...] = _silu(_dot(a, win_ref[:, 0:d]))
    forget = lb + (1.0 - lb) * _sigmoid(_dot(a, win_ref[:, d:2 * d]))
    lf_s[...] = jnp.log(forget)
    k_s[...] = 1.0 - forget
    v_s[...] = _dot(a, win_ref[:, 2 * d:3 * d]).astype(BF16)
    gate_s[...] = _silu(_dot(a, win_ref[:, 3 * d:4 * d]))
    causal = (lax.broadcasted_iota(jnp.int32, (c, c), 0) >= lax.broadcasted_iota(jnp.int32, (c, c), 1))
    tri = tri_ref[...]

    def chunk_body(ci, _):
        rows = pl.ds(pl.multiple_of(ci * c, c), c)
        lf = lf_s[rows]
        lf_hi = lf.astype(BF16)
        lf_lo = (lf - lf_hi.astype(F32)).astype(BF16)
        cum = _dot(tri, lf_hi) + _dot(tri, lf_lo)
        ref_row = cum[HGRN_REF:HGRN_REF + 1]
        last = cum[c - 1:c]
        q, k, v = q_s[rows], k_s[rows], v_s[rows]
        q_rel = (q * jnp.exp(cum - ref_row)).astype(BF16)
        k_rel = (k * jnp.exp(ref_row - cum)).astype(BF16)
        q_dec = (q * jnp.exp(cum)).astype(BF16)
        k_tail = (k * jnp.exp(last - cum)).astype(BF16)
        decay = jnp.exp(last)
        gate = gate_s[rows]
        heads = [slice(hd * dk, (hd + 1) * dk) for hd in range(d // dk)]
        attn = [jnp.where(causal, _dot_nt(q_rel[:, sl], k_rel[:, sl]), 0.0).astype(BF16) for sl in heads]
        inter = [_dot_nt(q_dec[:, sl], state_ref[hd].astype(BF16)) for hd, sl in enumerate(heads)]
        for hd, sl in enumerate(heads):
            state_ref[hd] = state_ref[hd] * decay[:, sl] + _dot_tn(v[:, sl], k_tail[:, sl])
        for hd, sl in enumerate(heads):
            o = _dot(attn[hd], v[:, sl]) + inter[hd]
            o_ref[rows, sl] = (_rms(o, go_ref[...]) * gate[:, sl]).astype(o_ref.dtype)
        return 0

    lax.fori_loop(0, tm // c, chunk_body, 0)


def _hgrn_mixer(h, g_mix, w_in, lb, g_o):
    b, lp, d = h.shape
    tm = ROW_TILE
    c = HGRN_CHUNK
    tri = (jnp.arange(c)[:, None] >= jnp.arange(c)[None, :]).astype(BF16)
    row_spec = pl.BlockSpec((None, tm, d), lambda i, t: (i, t, 0))
    small = [g_mix[None], w_in.astype(BF16), lb[None], g_o[None], tri]
    return pl.pallas_call(
        _hgrn_kernel,
        name='hgrn2',
        grid=(b, lp // tm),
        in_specs=[row_spec] + [_resident(w.shape) for w in small],
        out_specs=row_spec,
        out_shape=jax.ShapeDtypeStruct((b, lp, d), BF16),
        scratch_shapes=[pltpu.VMEM((d // HGRN_DK, d // HGRN_HEADS, HGRN_DK), F32),
                        pltpu.VMEM((tm, d), F32), pltpu.VMEM((tm, d), F32), pltpu.VMEM((tm, d), BF16),
                        pltpu.VMEM((tm, d), F32), pltpu.VMEM((tm, d), F32)],
        compiler_params=_cparams("arbitrary", "arbitrary"),
    )(h, *small)


def _norm_kernel(h_ref, g_ref, o_ref):
    o_ref[...] = _rms(h_ref[...], g_ref[...]).astype(o_ref.dtype)


def _s5_kernel(a_ref, perm_ref, op_ref, xm_ref, a1_ref, a2_ref, y_ref, w_s, yt_s):
    lp = a_ref.shape[0]
    t = S5_BLOCK
    nblk = lp // t
    gc = xm_ref.shape[0]
    width = 4 * S5_STATE
    half = width // 2
    steps = jnp.concatenate([a_ref[pl.ds(s, nblk, stride=t), :].astype(BF16) for s in range(t)], axis=1)
    u_all = _dot(steps, perm_ref[...]).astype(BF16)
    for g in range(gc):
        r = _dot(u_all[:, g * width:(g + 1) * width], op_ref[g])
        w_s[:, g * width:(g + 1) * width] = r[:, :width]
        yt_s[:, g * width:(g + 1) * width] = r[:, width:]
    a1, a2 = a1_ref[...], a2_ref[...]

    def swap_halves(z):
        parts = []
        for g in range(gc):
            parts += [z[:, g * width + half:(g + 1) * width], z[:, g * width:g * width + half]]
        return jnp.concatenate(parts, axis=1)

    def step(blk, z):
        row = pl.ds(blk, 1)
        w = w_s[row, :]
        w_s[row, :] = z
        return a1 * z + a2 * swap_halves(z) + w

    lax.fori_loop(0, nblk, step, jnp.zeros((1, gc * width), F32))
    ys = []
    for g in range(gc):
        x_start = w_s[:, g * width:g * width + half].astype(BF16)
        ys.append((yt_s[:, g * width:(g + 1) * width] + _dot(x_start, xm_ref[g])).astype(BF16))
    out = _dot_nt(jnp.concatenate(ys, axis=1), perm_ref[...])
    for s in range(t):
        y_ref[pl.ds(s, nblk, stride=t), :] = out[:, s * LANES:(s + 1) * LANES]


def _s5_operators(lam_re, lam_im, log_dt, b_re, b_im, c_re, c_im):
    hp = lax.Precision.HIGHEST
    t = S5_BLOCK
    dt = jnp.exp(log_dt)[:, None]
    mag = jnp.exp(lam_re * dt)
    abar_re, abar_im = mag * jnp.cos(lam_im * dt), mag * jnp.sin(lam_im * dt)
    den = lam_re * lam_re + lam_im * lam_im
    zoh_re = ((abar_re - 1.0) * lam_re + abar_im * lam_im) / den
    zoh_im = (abar_im * lam_re - (abar_re - 1.0) * lam_im) / den
    bb_re = zoh_re[..., None] * b_re - zoh_im[..., None] * b_im
    bb_im = zoh_re[..., None] * b_im + zoh_im[..., None] * b_re
    steps = jnp.arange(t + 1, dtype=F32)[:, None, None] * dt[None]
    pmag = jnp.exp(lam_re[None] * steps)
    pw_re, pw_im = pmag * jnp.cos(lam_im[None] * steps), pmag * jnp.sin(lam_im[None] * steps)
    cp_re = c_re[None] * pw_re[:t, :, None, :] - c_im[None] * pw_im[:t, :, None, :]
    cp_im = c_re[None] * pw_im[:t, :, None, :] + c_im[None] * pw_re[:t, :, None, :]
    taps = (jnp.einsum("tgjp,gpk->gktj", cp_re, bb_re, precision=hp)
            - jnp.einsum("tgjp,gpk->gktj", cp_im, bb_im, precision=hp))
    toe = jnp.stack([jnp.pad(taps[:, :, :t - s, :], ((0, 0), (0, 0), (s, 0), (0, 0))) for s in range(t)], axis=1)
    toe = toe.reshape(-1, t * S5_GROUP, t * S5_GROUP)
    rev_re, rev_im = pw_re[t - 1::-1][:t], pw_im[t - 1::-1][:t]
    w_re = rev_re[..., None] * bb_re[None] - rev_im[..., None] * bb_im[None]
    w_im = rev_re[..., None] * bb_im[None] + rev_im[..., None] * bb_re[None]
    to_rows = lambda m: m.transpose(1, 0, 3, 2).reshape(m.shape[1], t * S5_GROUP, S5_STATE)
    w_re, w_im = to_rows(w_re), to_rows(w_im)
    mw = jnp.concatenate([w_re, w_im, w_im, w_re], axis=-1)
    cq_re = c_re[None] * pw_re[1:, :, None, :] - c_im[None] * pw_im[1:, :, None, :]
    cq_im = c_re[None] * pw_im[1:, :, None, :] + c_im[None] * pw_re[1:, :, None, :]
    to_cols = lambda m: m.transpose(1, 3, 0, 2).reshape(m.shape[1], S5_STATE, t * S5_GROUP)
    xm = jnp.concatenate([to_cols(cq_re), -to_cols(cq_im)], axis=1)
    ar, ai = pw_re[t], pw_im[t]
    a1 = jnp.concatenate([ar, ar, ar, ar], axis=-1)
    a2 = jnp.concatenate([-ai, ai, ai, -ai], axis=-1)
    per_slab = lambda m: m.reshape(-1, 1, S5_SLAB_GROUPS * 4 * S5_STATE)
    ops = jnp.concatenate([mw, toe], axis=-1).astype(BF16)
    return ops, xm.astype(BF16), per_slab(a1), per_slab(a2)


def _s5_permutation():
    t, kk = S5_BLOCK, S5_GROUP
    row = jnp.arange(t * LANES)
    s, lane = row // LANES, row % LANES
    target = (lane // kk) * (t * kk) + s * kk + lane % kk
    return (target[:, None] == row[None, :]).astype(BF16)


def _s5_mixer(h, g_mix, lam_re, lam_im, log_dt, b_re, b_im, c_re, c_im):
    b, lp, d = h.shape
    t, kk = S5_BLOCK, S5_GROUP
    nblk = lp // t
    tm = ROW_TILE
    row_spec = pl.BlockSpec((None, tm, d), lambda i, j: (i, j, 0))
    a = pl.pallas_call(
        _norm_kernel,
        name='s5_norm',
        grid=(b, lp // tm),
        in_specs=[row_spec, _resident((1, d))],
        out_specs=row_spec,
        out_shape=jax.ShapeDtypeStruct((b, lp, d), F32),
        compiler_params=_cparams("parallel", "parallel"),
    )(h, g_mix[None])
    ops, xm, a1, a2 = _s5_operators(lam_re, lam_im, log_dt, b_re, b_im, c_re, c_im)
    sel = _s5_permutation()
    gc = S5_SLAB_GROUPS
    width = 4 * S5_STATE
    slab_spec = pl.BlockSpec((None, lp, LANES), lambda i, j: (j, 0, i))
    grp_spec = lambda r, c: pl.BlockSpec((gc, r, c), lambda i, j: (i, 0, 0))
    coef_spec = pl.BlockSpec((None, 1, gc * width), lambda i, j: (i, 0, 0))
    return pl.pallas_call(
        _s5_kernel,
        name='s5_scan',
        grid=(d // LANES, b),
        in_specs=[slab_spec, _resident(sel.shape), grp_spec(t * kk, width + t * kk),
                  grp_spec(2 * S5_STATE, t * kk), coef_spec, coef_spec],
        out_specs=slab_spec,
        out_shape=jax.ShapeDtypeStruct((b, lp, d), F32),
        scratch_shapes=[pltpu.VMEM((nblk, gc * width), F32), pltpu.VMEM((nblk, gc * width), F32)],
        compiler_params=_cparams("parallel", "parallel"),
    )(a, sel, ops, xm, a1, a2)


def _ret_kernel(h_ref, gmix_ref, win_ref, cos_ref, sin_ref, gn_ref, o_ref, state_ref, q_s, k_s, v_s, g_s):
    t = pl.program_id(1)
    tm, d = h_ref.shape
    nh = RET_HEADS
    dk = d // nh
    dv = 2 * dk
    c = RET_CHUNK
    half = dk // 2

    @pl.when(t == 0)
    def _():
        state_ref[...] = jnp.zeros_like(state_ref)

    a = _rms(h_ref[...], gmix_ref[...]).astype(BF16)
    cos, sin = cos_ref[...], sin_ref[...]
    for hd in range(nh):
        for ref, col0, scl in ((q_s, 0, 1.0), (k_s, d, dk ** -0.5)):
            x1 = _dot(a, win_ref[:, col0 + hd * dk:col0 + hd * dk + half])
            x2 = _dot(a, win_ref[:, col0 + hd * dk + half:col0 + (hd + 1) * dk])
            ref[:, hd * dk:hd * dk + half] = (x1 * cos - x2 * sin) * scl
            ref[:, hd * dk + half:(hd + 1) * dk] = (x1 * sin + x2 * cos) * scl
        v_s[:, hd * dv:(hd + 1) * dv] = _dot(a, win_ref[:, 2 * d + hd * dv:2 * d + (hd + 1) * dv]).astype(BF16)
        g_s[:, hd * dv:(hd + 1) * dv] = _silu(_dot(a, win_ref[:, 4 * d + hd * dv:4 * d + (hd + 1) * dv]))
    ri = lax.broadcasted_iota(jnp.int32, (c, c), 0)
    ci_ = lax.broadcasted_iota(jnp.int32, (c, c), 1)
    diff = (ri - ci_).astype(F32)
    pos = lax.broadcasted_iota(jnp.int32, (c, 1), 0).astype(F32)

    def chunk_body(cc, _):
        rows = pl.ds(pl.multiple_of(cc * c, c), c)
        log_g = [math.log(1.0 - 2.0 ** (-5.0 - hd)) for hd in range(nh)]
        scores, inter = [], []
        for hd, lg in enumerate(log_g):
            q = q_s[rows, hd * dk:(hd + 1) * dk]
            k = k_s[rows, hd * dk:(hd + 1) * dk]
            v = v_s[rows, hd * dv:(hd + 1) * dv]
            st = state_ref[hd]
            decay = jnp.where(diff >= 0, jnp.exp(diff * lg), 0.0)
            scores.append((_dot_nt(q.astype(BF16), k.astype(BF16)) * decay).astype(BF16))
            q_in = (q * jnp.exp((pos + 1.0) * lg)).astype(BF16)
            k_out = (k * jnp.exp((c - 1.0 - pos) * lg)).astype(BF16)
            inter.append(_dot(q_in, st.astype(BF16)))
            state_ref[hd] = st * math.exp(c * lg) + _dot_tn(k_out, v)
        for hd in range(nh):
            o = _dot(scores[hd], v_s[rows, hd * dv:(hd + 1) * dv]) + inter[hd]
            mu = jnp.mean(o, axis=-1, keepdims=True)
            oc = o - mu
            var = jnp.mean(oc * oc, axis=-1, keepdims=True)
            o = oc * lax.rsqrt(var + EPS) * gn_ref[:, hd * dv:(hd + 1) * dv] * g_s[rows, hd * dv:(hd + 1) * dv]
            o_ref[rows, hd * dv:(hd + 1) * dv] = o.astype(o_ref.dtype)
        return 0

    lax.fori_loop(0, tm // c, chunk_body, 0, unroll=True)


def _ret_mixer(h, g_mix, w_in, gn_g):
    b, lp, d = h.shape
    nh = RET_HEADS
    dk = d // nh
    tm = ROW_TILE
    pos = jnp.maximum(jnp.arange(lp, dtype=F32) - PAD, 0.0)
    inv_freq = 1.0 / (ROPE_BASE ** (jnp.arange(0, dk, 2, dtype=F32) / dk))
    ang = pos[:, None] * inv_freq[None, :]
    cos, sin = jnp.cos(ang), jnp.sin(ang)
    row_spec = lambda width: pl.BlockSpec((None, tm, width), lambda i, t: (i, t, 0))
    tab_spec = pl.BlockSpec((tm, dk // 2), lambda i, t: (t, 0))
    win = w_in.astype(BF16)
    return pl.pallas_call(
        _ret_kernel,
        name='retention',
        grid=(b, lp // tm),
        in_specs=[row_spec(d), _resident((1, d)), _resident(win.shape), tab_spec, tab_spec,
                  _resident((1, 2 * d))],
        out_specs=row_spec(2 * d),
        out_shape=jax.ShapeDtypeStruct((b, lp, 2 * d), BF16),
        scratch_shapes=[pltpu.VMEM((nh, dk, 2 * dk), F32), pltpu.VMEM((tm, d), F32), pltpu.VMEM((tm, d), F32),
                        pltpu.VMEM((tm, 2 * d), BF16), pltpu.VMEM((tm, 2 * d), F32)],
        compiler_params=_cparams("arbitrary", "arbitrary"),
    )(h, g_mix[None], win, cos, sin, gn_g[None])


def kernel(x, meta_tokens, norm_mix_g, norm_ffn_g, mla_w_down, mla_cq_norm_g, mla_ckv_norm_g, mla_w_uq, mla_w_ukv, mla_q_head_g, mla_k_head_g, mla_w_o, hgrn_w_in, hgrn_lb_logits, hgrn_o_norm_g, hgrn_w_o, s5_lam_re, s5_lam_im, s5_log_dt, s5_b_re, s5_b_im, s5_c_re, s5_c_im, s5_d, s5_w_glu, ret_w_in, ret_gn_g, ret_w_o, ffn_w_up, ffn_conv_w, ffn_conv_b, ffn_w_down):
    b, seq, d = x.shape
    depth = norm_mix_g.shape[0]
    h = jnp.concatenate([jnp.zeros((b, PAD, d), x.dtype),
                         jnp.broadcast_to(meta_tokens[None].astype(x.dtype), (b, N_META, d)), x], axis=1)
    lb_cum = jnp.cumsum(jax.nn.softmax(hgrn_lb_logits.astype(F32), axis=0), axis=0)
    lb_all = lb_cum - lb_cum[0:1]
    for i in range(depth):
        m, j = i % 4, i // 4
        ffn_w = (norm_ffn_g[i][None], ffn_w_up[i].astype(BF16), ffn_conv_w[i], ffn_conv_b[i][None],
                 ffn_w_down[i].astype(BF16))
        g_mix = norm_mix_g[i]
        if m == 0:
            o = _mla_mixer(h, g_mix, mla_w_down[j], mla_cq_norm_g[j], mla_ckv_norm_g[j], mla_w_uq[j],
                           mla_w_ukv[j], mla_q_head_g[j], mla_k_head_g[j])
            h = _ffn_after_proj(h, o, mla_w_o[j].astype(BF16), ffn_w)
        elif m == 1:
            o = _hgrn_mixer(h, g_mix, hgrn_w_in[j], lb_all[i], hgrn_o_norm_g[j])
            h = _ffn_after_proj(h, o, hgrn_w_o[j].astype(BF16), ffn_w)
        elif m == 2:
            y = _s5_mixer(h, g_mix, s5_lam_re[j], s5_lam_im[j], s5_log_dt[j], s5_b_re[j], s5_b_im[j],
                          s5_c_re[j], s5_c_im[j])
            h = _ffn_after_s5(h, y, g_mix[None], s5_d[j][None], s5_w_glu[j].astype(BF16), ffn_w)
        else:
            o = _ret_mixer(h, g_mix, ret_w_in[j], ret_gn_g[j])
            h = _ffn_after_proj(h, o, ret_w_o[j].astype(BF16), ffn_w)
    return h[:, FRONT:]
```

```python
import functools
import math

import jax
import jax.numpy as jnp
from jax import lax
from jax.experimental import pallas as pl
from jax.experimental.pallas import tpu as pltpu

F32 = jnp.float32
BF16 = jnp.bfloat16

N_META = 16
EPS = 1e-6
NEG_INF = -1e30
ROPE_BASE = 10000.0
MLA_HEADS = 8
MLA_NOPE = 128
MLA_ROPE = 64
MLA_V = 128
MLA_QK = MLA_NOPE + MLA_ROPE
MLA_Q_LORA = 384
MLA_KV_LORA = 256
HGRN_HEADS = 8
HGRN_DK = 128
S5_GROUP = 16
S5_STATE = 64
RET_HEADS = 4

LANES = 128
PAD = LANES - N_META
FRONT = PAD + N_META
ROW_TILE = 384
FFN_ROW_TILE = 704
ATT_BLOCK = 256
ATT_SUBS = 4
ATT_MAX_SHIFT = 48.0
HGRN_CHUNK = 64
HGRN_REF = HGRN_CHUNK // 2 - 1
RET_CHUNK = 128
S5_BLOCK = 16
S5_SLAB_GROUPS = LANES // S5_GROUP
FFN_COLS = 256
VMEM_LIMIT = 56 * 1024 * 1024


def _cparams(*sem):
    return pltpu.CompilerParams(dimension_semantics=sem, vmem_limit_bytes=VMEM_LIMIT)


def _resident(shape):
    nd = len(shape)
    return pl.BlockSpec(shape, lambda *_: (0,) * nd, pipeline_mode=pl.Buffered(1))


def _rms(x, g):
    return x * lax.rsqrt(jnp.mean(x * x, axis=-1, keepdims=True) + EPS) * g


def _dot(a, b):
    return jnp.dot(a, b, preferred_element_type=F32)


def _dot_nt(a, b):
    return lax.dot_general(a, b, (((1,), (1,)), ((), ())), preferred_element_type=F32)


def _dot_tn(a, b):
    return lax.dot_general(a, b, (((0,), (0,)), ((), ())), preferred_element_type=F32)


def _sigmoid(x):
    return 1.0 / (1.0 + jnp.exp(-x))


def _silu(x):
    return x * _sigmoid(x)


def _gelu_tanh(x):
    return 0.5 * x * (1.0 + jnp.tanh(math.sqrt(2.0 / math.pi) * (x + 0.044715 * (x * x * x))))


def _tile_valid(t, tm):
    return t * tm + lax.broadcasted_iota(jnp.int32, (tm, 1), 0) >= PAD


def _ffn_tail(x, t, g_ref, wup_ref, cw_ref, cb_ref, wdn_ref, o_ref, carry_ref, act_ref):
    tm = x.shape[0]
    hidden = wdn_ref.shape[0]
    valid = _tile_valid(t, tm)
    x = jnp.where(valid, x, 0.0)
    a = _rms(x, g_ref[...]).astype(BF16)
    rid = lax.broadcasted_iota(jnp.int32, (tm, 1), 0)

    @pl.when(t == 0)
    def _():
        carry_ref[...] = jnp.zeros_like(carry_ref)

    def conv_slice(col0):
        cols = slice(col0, col0 + FFN_COLS)
        u = _dot(a, wup_ref[:, cols])
        prev = carry_ref[:, cols]
        p0, p1 = prev[6:7], prev[7:8]
        u1 = jnp.where(rid == 0, p1, pltpu.roll(u, 1, 0))
        u2 = jnp.where(rid == 0, p0, jnp.where(rid == 1, p1, pltpu.roll(u, 2, 0)))
        carry_ref[:, cols] = u[tm - 8:tm]
        w = cw_ref[:, cols]
        return w[0:1] * u2 + w[1:2] * u1 + w[2:3] * u + cb_ref[:, cols]

    for c in range(hidden // FFN_COLS):
        gate = conv_slice(c * FFN_COLS)
        val = conv_slice(hidden + c * FFN_COLS)
        act_ref[:, c * FFN_COLS:(c + 1) * FFN_COLS] = (_silu(gate) * val).astype(BF16)
    y = _dot(act_ref[...], wdn_ref[...])
    o_ref[...] = jnp.where(valid, x + y, 0.0)


def _ffn_proj_kernel(h_ref, o_in_ref, wo_ref, g_ref, wup_ref, cw_ref, cb_ref, wdn_ref,
                     o_ref, carry_ref, act_ref):
    t = pl.program_id(1)
    x = h_ref[...] + _dot(o_in_ref[...], wo_ref[...])
    _ffn_tail(x, t, g_ref, wup_ref, cw_ref, cb_ref, wdn_ref, o_ref, carry_ref, act_ref)


def _ffn_s5_kernel(h_ref, y_ref, gmix_ref, d_ref, wglu_ref, g_ref, wup_ref, cw_ref, cb_ref, wdn_ref,
                   o_ref, carry_ref, act_ref):
    t = pl.program_id(1)
    h = h_ref[...]
    d_model = h.shape[1]
    a = _rms(h, gmix_ref[...])
    z = _gelu_tanh(y_ref[...].astype(F32) + d_ref[...] * a).astype(BF16)
    val = _dot(z, wglu_ref[:, :d_model])
    gate = _dot(z, wglu_ref[:, d_model:])
    x = h + val * _sigmoid(gate)
    _ffn_tail(x, t, g_ref, wup_ref, cw_ref, cb_ref, wdn_ref, o_ref, carry_ref, act_ref)


def _ffn_call(kernel, h, lead_inputs, lead_specs, ffn_w):
    b, lp, d = h.shape
    g, wup, cw, cb, wdn = ffn_w
    hidden = wdn.shape[0]
    tm = FFN_ROW_TILE if lp % FFN_ROW_TILE == 0 else ROW_TILE
    row_spec = lambda width: pl.BlockSpec((None, tm, width), lambda i, t: (i, t, 0))
    return pl.pallas_call(
        kernel,
        name=kernel.__name__.strip('_'),
        grid=(b, lp // tm),
        in_specs=[row_spec(d)] + lead_specs(row_spec) + [
            _resident(g.shape), _resident(wup.shape), _resident(cw.shape), _resident(cb.shape),
            _resident(wdn.shape)],
        out_specs=row_spec(d),
        out_shape=jax.ShapeDtypeStruct((b, lp, d), F32),
        scratch_shapes=[pltpu.VMEM((8, 2 * hidden), F32), pltpu.VMEM((tm, hidden), BF16)],
        compiler_params=_cparams("arbitrary", "arbitrary"),
    )(h, *lead_inputs, g, wup, cw, cb, wdn)


def _ffn_after_proj(h, o_in, wo, ffn_w):
    return _ffn_call(_ffn_proj_kernel, h, (o_in, wo),
                     lambda row_spec: [row_spec(o_in.shape[-1]), _resident(wo.shape)], ffn_w)


def _ffn_after_s5(h, y, gmix, dskip, wglu, ffn_w):
    return _ffn_call(_ffn_s5_kernel, h, (y, gmix, dskip, wglu),
                     lambda row_spec: [row_spec(y.shape[-1]), _resident(gmix.shape), _resident(dskip.shape),
                                       _resident(wglu.shape)], ffn_w)


def _rope_swap(r):
    lane = lax.broadcasted_iota(jnp.int32, r.shape, 1)
    half = MLA_ROPE // 2
    return jnp.where(lane < half, pltpu.roll(r, LANES - half, 1), pltpu.roll(r, half, 1))


def _mla_qkv_kernel(h_ref, gmix_ref, wd_ref, gcq_ref, gckv_ref, wuq_ref, wukv_ref,
                    gqn_ref, gqr_ref, gkn_ref, gkr_ref, qone_ref, kshift_ref, cos_ref, sin_ref,
                    q_ref, k_ref, v_ref):
    a = _rms(h_ref[...], gmix_ref[...]).astype(BF16)
    down = _dot(a, wd_ref[...])
    kv_lo = MLA_Q_LORA + MLA_KV_LORA
    cq = _rms(down[:, :MLA_Q_LORA], gcq_ref[...]).astype(BF16)
    ckv = _rms(down[:, MLA_Q_LORA:kv_lo], gckv_ref[...]).astype(BF16)
    kpe = down[:, kv_lo:kv_lo + LANES]
    q = _dot(cq, wuq_ref[...])
    kv = _dot(ckv, wukv_ref[...])
    cos, sin = cos_ref[...], sin_ref[...]

    def rope(r):
        return r * cos + _rope_swap(r) * sin

    ss_pe = jnp.sum(kpe * kpe, axis=-1, keepdims=True)
    k_rope = rope(kpe * gkr_ref[...])
    scale = MLA_QK ** -0.5 * math.log2(math.e)
    hw = 2 * LANES
    for hd in range(MLA_HEADS):
        qn = q[:, hd * hw:hd * hw + LANES]
        qr = q[:, hd * hw + LANES:(hd + 1) * hw]
        inv_q = lax.rsqrt(jnp.sum(qn * qn + qr * qr, axis=-1, keepdims=True) / MLA_QK + EPS) * scale
        q_ref[:, hd * hw:hd * hw + LANES] = (qn * inv_q * gqn_ref[...]).astype(BF16)
        q_ref[:, hd * hw + LANES:(hd + 1) * hw] = (rope(qr * gqr_ref[...]) * inv_q + qone_ref[...]).astype(BF16)
        kn = kv[:, hd * hw:hd * hw + LANES]
        inv_k = lax.rsqrt((jnp.sum(kn * kn, axis=-1, keepdims=True) + ss_pe) / MLA_QK + EPS)
        k_ref[:, hd * hw:hd * hw + LANES] = (kn * inv_k * gkn_ref[...]).astype(BF16)
        k_ref[:, hd * hw + LANES:(hd + 1) * hw] = (k_rope * inv_k + kshift_ref[...]).astype(BF16)
        v_ref[:, hd * LANES:(hd + 1) * LANES] = kv[:, hd * hw + LANES:(hd + 1) * hw].astype(BF16)


def _attn_kernel(q_ref, k_ref, v_ref, o_ref, ve_s):
    lp, dv = v_ref.shape
    blk, subs = ATT_BLOCK, ATT_SUBS
    n_sup = (lp - FRONT) // (blk * subs)
    ve_s[:, 0:dv] = v_ref[...]
    ve_s[:, dv:2 * dv] = jnp.ones((lp, dv), BF16)
    k0, v0 = k_ref[0:FRONT], ve_s[0:FRONT]
    key_ok0 = lax.broadcasted_iota(jnp.int32, (1, FRONT), 1) >= PAD
    chunk = FRONT // 2
    diag_ok = (lax.broadcasted_iota(jnp.int32, (blk, blk), 0) // chunk
               >= lax.broadcasted_iota(jnp.int32, (blk, blk), 1) // chunk)

    def first_block(q):
        s = jnp.where(key_ok0, _dot_nt(q, k0), NEG_INF)
        m = jnp.max(s, axis=-1, keepdims=True)
        return m, _dot(jnp.exp2(s - m).astype(BF16), v0)

    def update(carry, s, vb):
        m, acc = carry
        m_new = jnp.maximum(m, jnp.max(s, axis=-1, keepdims=True))
        p = jnp.exp2(s - m_new).astype(BF16)
        return m_new, jnp.exp2(m - m_new) * acc + _dot(p, vb)

    def finish(carry):
        acc = carry[1]
        return (acc[:, 0:dv] / acc[:, dv:2 * dv]).astype(o_ref.dtype)

    o_ref[0:FRONT] = finish(first_block(q_ref[0:FRONT]))

    def sup_body(i, _):
        base = FRONT + i * (blk * subs)
        rows = [pl.ds(pl.multiple_of(base + a * blk, LANES), blk) for a in range(subs)]

        def key_rows(j):
            return pl.ds(pl.multiple_of(FRONT + j * blk, LANES), blk)

        def scores(j):
            kb = k_ref[key_rows(j)]
            return tuple(_dot_nt(q_ref[rows[a]], kb) for a in range(subs))

        def kv_body(j, state):
            carries, s_cur = state
            s_next = scores(j + 1)
            vb = ve_s[key_rows(j)]
            return tuple(update(carries[a], s_cur[a], vb) for a in range(subs)), s_next

        carries, s_cur = lax.fori_loop(0, i * subs, kv_body,
                                       (tuple(first_block(q_ref[r]) for r in rows), scores(0)))
        carries = list(carries)
        for c in range(subs):
            if c > 0:
                kb = k_ref[rows[c]]
                s_cur = [None] * c + [_dot_nt(q_ref[rows[a]], kb) for a in range(c, subs)]
            vb = ve_s[rows[c]]
            for a in range(c, subs):
                s = jnp.where(diag_ok, s_cur[a], NEG_INF) if a == c else s_cur[a]
                carries[a] = update(carries[a], s, vb)
        for a in range(subs):
            o_ref[rows[a]] = finish(carries[a])
        return 0

    lax.fori_loop(0, n_sup, sup_body, 0)


def _attn_bounded_kernel(q_ref, k_ref, v_ref, o_ref, ve_s):
    lp, dv = v_ref.shape
    blk, subs = ATT_BLOCK, ATT_SUBS
    n_sup = (lp - FRONT) // (blk * subs)
    ve_s[:, 0:dv] = v_ref[...]
    ve_s[:, dv:2 * dv] = jnp.ones((lp, dv), BF16)
    k0, v0 = k_ref[0:FRONT], ve_s[0:FRONT]
    key_ok0 = lax.broadcasted_iota(jnp.int32, (1, FRONT), 1) >= PAD
    chunk = FRONT // 2
    diag_ok = (lax.broadcasted_iota(jnp.int32, (blk, blk), 0) // chunk
               >= lax.broadcasted_iota(jnp.int32, (blk, blk), 1) // chunk)

    def weigh(s, vb, ok=None):
        if ok is not None:
            s = jnp.where(ok, s, NEG_INF)
        return _dot(jnp.exp2(s).astype(BF16), vb)

    def finish(acc):
        return (acc[:, 0:dv] / acc[:, dv:2 * dv]).astype(o_ref.dtype)

    o_ref[0:FRONT] = finish(weigh(_dot_nt(q_ref[0:FRONT], k0), v0, key_ok0))

    def sup_body(i, _):
        base = FRONT + i * (blk * subs)
        rows = [pl.ds(pl.multiple_of(base + a * blk, LANES), blk) for a in range(subs)]

        def key_rows(j):
            return pl.ds(pl.multiple_of(FRONT + j * 2 * blk, LANES), 2 * blk)

        def scores(j, first=0):
            kb = k_ref[key_rows(j)]
            return [_dot_nt(q_ref[rows[a]], kb) for a in range(first, subs)]

        def kv_body(j, state):
            accs, s_cur = state
            s_next = scores(j + 1)
            vb = ve_s[key_rows(j)]
            parts = [weigh(s, vb) for s in s_cur]
            return tuple(acc + part for acc, part in zip(accs, parts)), tuple(s_next)

        s0 = [_dot_nt(q_ref[r], k0) for r in rows]
        n_pairs = i * (subs // 2)
        accs, s_cur = lax.fori_loop(0, n_pairs, kv_body,
                                    (tuple(weigh(s, v0, key_ok0) for s in s0), tuple(scores(0))))
        accs = list(accs)
        q_chunk = lax.broadcasted_iota(jnp.int32, (blk, 2 * blk), 0) // chunk
        k_chunk = lax.broadcasted_iota(jnp.int32, (blk, 2 * blk), 1) // chunk
        for pair in range(subs // 2):
            first = 2 * pair
            if pair > 0:
                s_cur = scores(n_pairs + pair, first=first)
            vb = ve_s[key_rows(n_pairs + pair)]
            parts = []
            for n, s in enumerate(s_cur[-(subs - first):]):
                ok = (k_chunk <= q_chunk + n * (blk // chunk)) if n < 2 else None
                parts.append(weigh(s, vb, ok))
            for n, part in enumerate(parts):
                accs[first + n] = accs[first + n] + part
        for a in range(subs):
            o_ref[rows[a]] = finish(accs[a])
        return 0

    lax.fori_loop(0, n_sup, sup_body, 0)


def _mla_rope_tables(lp):
    half = MLA_ROPE // 2
    pos = jnp.maximum(jnp.arange(lp, dtype=F32) - PAD, 0.0)
    inv_freq = 1.0 / (ROPE_BASE ** (jnp.arange(0, MLA_ROPE, 2, dtype=F32) / MLA_ROPE))
    ang = pos[:, None] * inv_freq[None, :]
    c, s = jnp.cos(ang), jnp.sin(ang)
    z = jnp.zeros((lp, LANES - MLA_ROPE), F32)
    return jnp.concatenate([c, c, z], axis=1), jnp.concatenate([-s, s, z], axis=1)


def _mla_mixer(h, g_mix, w_down, g_cq, g_ckv, w_uq, w_ukv, g_qhead, g_khead):
    b, lp, d = h.shape
    nh, hw = MLA_HEADS, 2 * LANES
    kv_lo = MLA_Q_LORA + MLA_KV_LORA
    wd = jnp.pad(w_down, ((0, 0), (0, kv_lo + LANES - w_down.shape[1]))).astype(BF16)
    wuq = jnp.pad(w_uq.reshape(MLA_Q_LORA, nh, MLA_QK), ((0, 0), (0, 0), (0, hw - MLA_QK)))
    wuq = wuq.reshape(MLA_Q_LORA, nh * hw).astype(BF16)
    wukv = w_ukv.astype(BF16)
    zpad = jnp.zeros((LANES - MLA_ROPE,), F32)
    gqn, gkn = g_qhead[None, :MLA_NOPE], g_khead[None, :MLA_NOPE]
    gqr = jnp.concatenate([g_qhead[MLA_NOPE:], zpad])[None]
    gkr = jnp.concatenate([g_khead[MLA_NOPE:], zpad])[None]
    cos, sin = _mla_rope_tables(lp)
    shift = (math.log2(math.e) * math.sqrt(MLA_QK)) * jnp.max(jnp.abs(g_qhead)) * jnp.max(jnp.abs(g_khead))
    spare = jnp.arange(LANES) == MLA_ROPE
    qone = jnp.where(spare, 1.0, 0.0).astype(F32)[None]
    kshift = jnp.where(spare, -shift, 0.0).astype(F32)[None]
    tm = ROW_TILE
    row_spec = lambda width: pl.BlockSpec((None, tm, width), lambda i, t: (i, t, 0))
    tab_spec = pl.BlockSpec((tm, LANES), lambda i, t: (t, 0))
    small = [g_mix[None], wd, g_cq[None], g_ckv[None], wuq, wukv, gqn, gqr, gkn, gkr, qone, kshift]
    q, k, v = pl.pallas_call(
        _mla_qkv_kernel,
        name='mla_qkv',
        grid=(b, lp // tm),
        in_specs=[row_spec(d)] + [_resident(w.shape) for w in small] + [tab_spec, tab_spec],
        out_specs=[row_spec(nh * hw), row_spec(nh * hw), row_spec(nh * MLA_V)],
        out_shape=[jax.ShapeDtypeStruct((b, lp, nh * hw), BF16), jax.ShapeDtypeStruct((b, lp, nh * hw), BF16),
                   jax.ShapeDtypeStruct((b, lp, nh * MLA_V), BF16)],
        compiler_params=_cparams("parallel", "parallel"),
    )(h, *small, cos, sin)
    head_spec = lambda width: pl.BlockSpec((None, lp, width), lambda i, j: (i, 0, j))

    def attend(body, name):
        return pl.pallas_call(
            body,
            name=name,
            grid=(b, nh),
            in_specs=[head_spec(hw), head_spec(hw), head_spec(MLA_V)],
            out_specs=head_spec(MLA_V),
            out_shape=jax.ShapeDtypeStruct((b, lp, nh * MLA_V), BF16),
            scratch_shapes=[pltpu.VMEM((lp, 2 * MLA_V), BF16)],
            compiler_params=_cparams("parallel", "parallel"),
        )

    return lax.cond(shift <= ATT_MAX_SHIFT,
                    attend(_attn_bounded_kernel, 'mla_attention_bounded'),
                    attend(_attn_kernel, 'mla_attention'), q, k, v)


def _hgrn_kernel(h_ref, gmix_ref, win_ref, lb_ref, go_ref, tri_ref, o_ref,
                 state_ref, q_s, k_s, v_s, lf_s, gate_s):
    t = pl.program_id(1)
    tm, d = h_ref.shape
    dk = HGRN_DK
    c = HGRN_CHUNK

    @pl.when(t == 0)
    def _():
        state_ref[...] = jnp.zeros_like(state_ref)

    a = _rms(h_ref[...], gmix_ref[...]).astype(BF16)
    lb = lb_ref[...]
    q_s[...] = _silu(_dot(a, win_ref[:, 0:d]))
    forget = lb + (1.0 - lb) * _sigmoid(_dot(a, win_ref[:, d:2 * d]))
    lf_s[...] = jnp.log(forget)
    k_s[...] = 1.0 - forget
    v_s[...] = _dot(a, win_ref[:, 2 * d:3 * d]).astype(BF16)
    gate_s[...] = _silu(_dot(a, win_ref[:, 3 * d:4 * d]))
    causal = (lax.broadcasted_iota(jnp.int32, (c, c), 0) >= lax.broadcasted_iota(jnp.int32, (c, c), 1))
    tri = tri_ref[...]

    def chunk_body(ci, _):
        rows = pl.ds(pl.multiple_of(ci * c, c), c)
        lf = lf_s[rows]
        lf_hi = lf.astype(BF16)
        lf_lo = (lf - lf_hi.astype(F32)).astype(BF16)
        cum = _dot(tri, lf_hi) + _dot(tri, lf_lo)
        ref_row = cum[HGRN_REF:HGRN_REF + 1]
        last = cum[c - 1:c]
        q, k, v = q_s[rows], k_s[rows], v_s[rows]
        q_rel = (q * jnp.exp(cum - ref_row)).astype(BF16)
        k_rel = (k * jnp.exp(ref_row - cum)).astype(BF16)
        q_dec = (q * jnp.exp(cum)).astype(BF16)
        k_tail = (k * jnp.exp(last - cum)).astype(BF16)
        decay = jnp.exp(last)
        gate = gate_s[rows]
        heads = [slice(hd * dk, (hd + 1) * dk) for hd in range(d // dk)]
        attn = [jnp.where(causal, _dot_nt(q_rel[:, sl], k_rel[:, sl]), 0.0).astype(BF16) for sl in heads]
        inter = [_dot_nt(q_dec[:, sl], state_ref[hd].astype(BF16)) for hd, sl in enumerate(heads)]
        for hd, sl in enumerate(heads):
            state_ref[hd] = state_ref[hd] * decay[:, sl] + _dot_tn(v[:, sl], k_tail[:, sl])
        for hd, sl in enumerate(heads):
            o = _dot(attn[hd], v[:, sl]) + inter[hd]
            o_ref[rows, sl] = (_rms(o, go_ref[...]) * gate[:, sl]).astype(o_ref.dtype)
        return 0

    lax.fori_loop(0, tm // c, chunk_body, 0)


def _hgrn_mixer(h, g_mix, w_in, lb, g_o):
    b, lp, d = h.shape
    tm = ROW_TILE
    c = HGRN_CHUNK
    tri = (jnp.arange(c)[:, None] >= jnp.arange(c)[None, :]).astype(BF16)
    row_spec = pl.BlockSpec((None, tm, d), lambda i, t: (i, t, 0))
    small = [g_mix[None], w_in.astype(BF16), lb[None], g_o[None], tri]
    return pl.pallas_call(
        _hgrn_kernel,
        name='hgrn2',
        grid=(b, lp // tm),
        in_specs=[row_spec] + [_resident(w.shape) for w in small],
        out_specs=row_spec,
        out_shape=jax.ShapeDtypeStruct((b, lp, d), BF16),
        scratch_shapes=[pltpu.VMEM((d // HGRN_DK, d // HGRN_HEADS, HGRN_DK), F32),
                        pltpu.VMEM((tm, d), F32), pltpu.VMEM((tm, d), F32), pltpu.VMEM((tm, d), BF16),
                        pltpu.VMEM((tm, d), F32), pltpu.VMEM((tm, d), F32)],
        compiler_params=_cparams("arbitrary", "arbitrary"),
    )(h, *small)


def _norm_kernel(h_ref, g_ref, o_ref):
    o_ref[...] = _rms(h_ref[...], g_ref[...]).astype(o_ref.dtype)


def _s5_kernel(a_ref, perm_ref, mw_ref, toe_ref, xm_ref, a1_ref, a2_ref, y_ref, w_s, yt_s):
    lp = a_ref.shape[0]
    t = S5_BLOCK
    nblk = lp // t
    gc = xm_ref.shape[0]
    width = 4 * S5_STATE
    half = width // 2
    steps = jnp.concatenate([a_ref[pl.ds(s, nblk, stride=t), :].astype(BF16) for s in range(t)], axis=1)
    u_all = _dot(steps, perm_ref[...]).astype(BF16)
    for g in range(gc):
        u = u_all[:, g * width:(g + 1) * width]
        w_s[:, g * width:(g + 1) * width] = _dot(u, mw_ref[g])
        yt_s[:, g * width:(g + 1) * width] = _dot(u, toe_ref[g])
    a1, a2 = a1_ref[...], a2_ref[...]

    def swap_halves(z):
        parts = []
        for g in range(gc):
            parts += [z[:, g * width + half:(g + 1) * width], z[:, g * width:g * width + half]]
        return jnp.concatenate(parts, axis=1)

    def step(blk, z):
        row = pl.ds(blk, 1)
        w = w_s[row, :]
        w_s[row, :] = z
        return a1 * z + a2 * swap_halves(z) + w

    lax.fori_loop(0, nblk, step, jnp.zeros((1, gc * width), F32))
    ys = []
    for g in range(gc):
        x_start = w_s[:, g * width:g * width + half].astype(BF16)
        ys.append((yt_s[:, g * width:(g + 1) * width] + _dot(x_start, xm_ref[g])).astype(BF16))
    out = _dot_nt(jnp.concatenate(ys, axis=1), perm_ref[...])
    for s in range(t):
        y_ref[pl.ds(s, nblk, stride=t), :] = out[:, s * LANES:(s + 1) * LANES]


def _s5_operators(lam_re, lam_im, log_dt, b_re, b_im, c_re, c_im):
    hp = lax.Precision.HIGHEST
    t = S5_BLOCK
    dt = jnp.exp(log_dt)[:, None]
    mag = jnp.exp(lam_re * dt)
    abar_re, abar_im = mag * jnp.cos(lam_im * dt), mag * jnp.sin(lam_im * dt)
    den = lam_re * lam_re + lam_im * lam_im
    zoh_re = ((abar_re - 1.0) * lam_re + abar_im * lam_im) / den
    zoh_im = (abar_im * lam_re - (abar_re - 1.0) * lam_im) / den
    bb_re = zoh_re[..., None] * b_re - zoh_im[..., None] * b_im
    bb_im = zoh_re[..., None] * b_im + zoh_im[..., None] * b_re
    steps = jnp.arange(t + 1, dtype=F32)[:, None, None] * dt[None]
    pmag = jnp.exp(lam_re[None] * steps)
    pw_re, pw_im = pmag * jnp.cos(lam_im[None] * steps), pmag * jnp.sin(lam_im[None] * steps)
    cp_re = c_re[None] * pw_re[:t, :, None, :] - c_im[None] * pw_im[:t, :, None, :]
    cp_im = c_re[None] * pw_im[:t, :, None, :] + c_im[None] * pw_re[:t, :, None, :]
    taps = jnp.einsum("tgjp,gpk->gktj", jnp.concatenate([cp_re, -cp_im], axis=-1),
                      jnp.concatenate([bb_re, bb_im], axis=1), precision=hp)
    ar_t = jnp.arange(t)
    place = (ar_t[None, :, None] - ar_t[:, None, None] == ar_t[None, None, :]).astype(BF16)
    toe = jnp.einsum("stu,gkuj->gsktj", place, taps.astype(BF16)).reshape(-1, t * S5_GROUP, t * S5_GROUP)
    rev_re, rev_im = pw_re[t - 1::-1][:t], pw_im[t - 1::-1][:t]
    w_re = rev_re[..., None] * bb_re[None] - rev_im[..., None] * bb_im[None]
    w_im = rev_re[..., None] * bb_im[None] + rev_im[..., None] * bb_re[None]
    to_rows = lambda m: m.transpose(1, 0, 3, 2).reshape(m.shape[1], t * S5_GROUP, S5_STATE)
    w_re, w_im = to_rows(w_re), to_rows(w_im)
    mw = jnp.concatenate([w_re, w_im, w_im, w_re], axis=-1)
    cq_re = c_re[None] * pw_re[1:, :, None, :] - c_im[None] * pw_im[1:, :, None, :]
    cq_im = c_re[None] * pw_im[1:, :, None, :] + c_im[None] * pw_re[1:, :, None, :]
    to_cols = lambda m: m.transpose(1, 3, 0, 2).reshape(m.shape[1], S5_STATE, t * S5_GROUP)
    xm = jnp.concatenate([to_cols(cq_re), -to_cols(cq_im)], axis=1)
    ar, ai = pw_re[t], pw_im[t]
    a1 = jnp.concatenate([ar, ar, ar, ar], axis=-1)
    a2 = jnp.concatenate([-ai, ai, ai, -ai], axis=-1)
    per_slab = lambda m: m.reshape(-1, 1, S5_SLAB_GROUPS * 4 * S5_STATE)
    return mw.astype(BF16), toe, xm.astype(BF16), per_slab(a1), per_slab(a2)


def _s5_permutation():
    t, kk = S5_BLOCK, S5_GROUP
    row = jnp.arange(t * LANES)
    s, lane = row // LANES, row % LANES
    target = (lane // kk) * (t * kk) + s * kk + lane % kk
    return (target[:, None] == row[None, :]).astype(BF16)


def _s5_mixer(h, g_mix, lam_re, lam_im, log_dt, b_re, b_im, c_re, c_im):
    b, lp, d = h.shape
    t, kk = S5_BLOCK, S5_GROUP
    nblk = lp // t
    tm = ROW_TILE
    row_spec = pl.BlockSpec((None, tm, d), lambda i, j: (i, j, 0))
    a = pl.pallas_call(
        _norm_kernel,
        name='s5_norm',
        grid=(b, lp // tm),
        in_specs=[row_spec, _resident((1, d))],
        out_specs=row_spec,
        out_shape=jax.ShapeDtypeStruct((b, lp, d), F32),
        compiler_params=_cparams("parallel", "parallel"),
    )(h, g_mix[None])
    mw, toe, xm, a1, a2 = _s5_operators(lam_re, lam_im, log_dt, b_re, b_im, c_re, c_im)
    sel = _s5_permutation()
    gc = S5_SLAB_GROUPS
    width = 4 * S5_STATE
    slab_spec = pl.BlockSpec((None, lp, LANES), lambda i, j: (j, 0, i))
    grp_spec = lambda r, c: pl.BlockSpec((gc, r, c), lambda i, j: (i, 0, 0))
    coef_spec = pl.BlockSpec((None, 1, gc * width), lambda i, j: (i, 0, 0))
    return pl.pallas_call(
        _s5_kernel,
        name='s5_scan',
        grid=(d // LANES, b),
        in_specs=[slab_spec, _resident(sel.shape), grp_spec(t * kk, width), grp_spec(t * kk, t * kk),
                  grp_spec(2 * S5_STATE, t * kk), coef_spec, coef_spec],
        out_specs=slab_spec,
        out_shape=jax.ShapeDtypeStruct((b, lp, d), F32),
        scratch_shapes=[pltpu.VMEM((nblk, gc * width), F32), pltpu.VMEM((nblk, gc * width), F32)],
        compiler_params=_cparams("parallel", "parallel"),
    )(a, sel, mw, toe, xm, a1, a2)


def _ret_kernel(h_ref, gmix_ref, win_ref, cos_ref, sin_ref, gn_ref, o_ref, state_ref, q_s, k_s, v_s, g_s):
    t = pl.program_id(1)
    tm, d = h_ref.shape
    nh = RET_HEADS
    dk = d // nh
    dv = 2 * dk
    c = RET_CHUNK
    half = dk // 2

    @pl.when(t == 0)
    def _():
        state_ref[...] = jnp.zeros_like(state_ref)

    a = _rms(h_ref[...], gmix_ref[...]).astype(BF16)
    cos, sin = cos_ref[...], sin_ref[...]
    for hd in range(nh):
        for ref, col0, scl in ((q_s, 0, 1.0), (k_s, d, dk ** -0.5)):
            x1 = _dot(a, win_ref[:, col0 + hd * dk:col0 + hd * dk + half])
            x2 = _dot(a, win_ref[:, col0 + hd * dk + half:col0 + (hd + 1) * dk])
            ref[:, hd * dk:hd * dk + half] = (x1 * cos - x2 * sin) * scl
            ref[:, hd * dk + half:(hd + 1) * dk] = (x1 * sin + x2 * cos) * scl
        v_s[:, hd * dv:(hd + 1) * dv] = _dot(a, win_ref[:, 2 * d + hd * dv:2 * d + (hd + 1) * dv]).astype(BF16)
        g_s[:, hd * dv:(hd + 1) * dv] = _silu(_dot(a, win_ref[:, 4 * d + hd * dv:4 * d + (hd + 1) * dv]))
    ri = lax.broadcasted_iota(jnp.int32, (c, c), 0)
    ci_ = lax.broadcasted_iota(jnp.int32, (c, c), 1)
    diff = (ri - ci_).astype(F32)
    pos = lax.broadcasted_iota(jnp.int32, (c, 1), 0).astype(F32)

    def chunk_body(cc, _):
        rows = pl.ds(pl.multiple_of(cc * c, c), c)
        log_g = [math.log(1.0 - 2.0 ** (-5.0 - hd)) for hd in range(nh)]
        scores, inter = [], []
        for hd, lg in enumerate(log_g):
            q = q_s[rows, hd * dk:(hd + 1) * dk]
            k = k_s[rows, hd * dk:(hd + 1) * dk]
            v = v_s[rows, hd * dv:(hd + 1) * dv]
            st = state_ref[hd]
            decay = jnp.where(diff >= 0, jnp.exp(diff * lg), 0.0)
            scores.append((_dot_nt(q.astype(BF16), k.astype(BF16)) * decay).astype(BF16))
            q_in = (q * jnp.exp((pos + 1.0) * lg)).astype(BF16)
            k_out = (k * jnp.exp((c - 1.0 - pos) * lg)).astype(BF16)
            inter.append(_dot(q_in, st.astype(BF16)))
            state_ref[hd] = st * math.exp(c * lg) + _dot_tn(k_out, v)
        for hd in range(nh):
            o = _dot(scores[hd], v_s[rows, hd * dv:(hd + 1) * dv]) + inter[hd]
            mu = jnp.mean(o, axis=-1, keepdims=True)
            oc = o - mu
            var = jnp.mean(oc * oc, axis=-1, keepdims=True)
            o = oc * lax.rsqrt(var + EPS) * gn_ref[:, hd * dv:(hd + 1) * dv] * g_s[rows, hd * dv:(hd + 1) * dv]
            o_ref[rows, hd * dv:(hd + 1) * dv] = o.astype(o_ref.dtype)
        return 0

    lax.fori_loop(0, tm // c, chunk_body, 0, unroll=True)


def _ret_mixer(h, g_mix, w_in, gn_g):
    b, lp, d = h.shape
    nh = RET_HEADS
    dk = d // nh
    tm = ROW_TILE
    pos = jnp.maximum(jnp.arange(lp, dtype=F32) - PAD, 0.0)
    inv_freq = 1.0 / (ROPE_BASE ** (jnp.arange(0, dk, 2, dtype=F32) / dk))
    ang = pos[:, None] * inv_freq[None, :]
    cos, sin = jnp.cos(ang), jnp.sin(ang)
    row_spec = lambda width: pl.BlockSpec((None, tm, width), lambda i, t: (i, t, 0))
    tab_spec = pl.BlockSpec((tm, dk // 2), lambda i, t: (t, 0))
    win = w_in.astype(BF16)
    return pl.pallas_call(
        _ret_kernel,
        name='retention',
        grid=(b, lp // tm),
        in_specs=[row_spec(d), _resident((1, d)), _resident(win.shape), tab_spec, tab_spec,
                  _resident((1, 2 * d))],
        out_specs=row_spec(2 * d),
        out_shape=jax.ShapeDtypeStruct((b, lp, 2 * d), BF16),
        scratch_shapes=[pltpu.VMEM((nh, dk, 2 * dk), F32), pltpu.VMEM((tm, d), F32), pltpu.VMEM((tm, d), F32),
                        pltpu.VMEM((tm, 2 * d), BF16), pltpu.VMEM((tm, 2 * d), F32)],
        compiler_params=_cparams("arbitrary", "arbitrary"),
    )(h, g_mix[None], win, cos, sin, gn_g[None])


def kernel(x, meta_tokens, norm_mix_g, norm_ffn_g, mla_w_down, mla_cq_norm_g, mla_ckv_norm_g, mla_w_uq, mla_w_ukv, mla_q_head_g, mla_k_head_g, mla_w_o, hgrn_w_in, hgrn_lb_logits, hgrn_o_norm_g, hgrn_w_o, s5_lam_re, s5_lam_im, s5_log_dt, s5_b_re, s5_b_im, s5_c_re, s5_c_im, s5_d, s5_w_glu, ret_w_in, ret_gn_g, ret_w_o, ffn_w_up, ffn_conv_w, ffn_conv_b, ffn_w_down):
    b, seq, d = x.shape
    depth = norm_mix_g.shape[0]
    h = jnp.concatenate([jnp.zeros((b, PAD, d), x.dtype),
                         jnp.broadcast_to(meta_tokens[None].astype(x.dtype), (b, N_META, d)), x], axis=1)
    lb_cum = jnp.cumsum(jax.nn.softmax(hgrn_lb_logits.astype(F32), axis=0), axis=0)
    lb_all = lb_cum - lb_cum[0:1]
    for i in range(depth):
        m, j = i % 4, i // 4
        ffn_w = (norm_ffn_g[i][None], ffn_w_up[i].astype(BF16), ffn_conv_w[i], ffn_conv_b[i][None],
                 ffn_w_down[i].astype(BF16))
        g_mix = norm_mix_g[i]
        if m == 0:
            o = _mla_mixer(h, g_mix, mla_w_down[j], mla_cq_norm_g[j], mla_ckv_norm_g[j], mla_w_uq[j],
                           mla_w_ukv[j], mla_q_head_g[j], mla_k_head_g[j])
            h = _ffn_after_proj(h, o, mla_w_o[j].astype(BF16), ffn_w)
        elif m == 1:
            o = _hgrn_mixer(h, g_mix, hgrn_w_in[j], lb_all[i], hgrn_o_norm_g[j])
            h = _ffn_after_proj(h, o, hgrn_w_o[j].astype(BF16), ffn_w)
        elif m == 2:
            y = _s5_mixer(h, g_mix, s5_lam_re[j], s5_lam_im[j], s5_log_dt[j], s5_b_re[j], s5_b_im[j],
                          s5_c_re[j], s5_c_im[j])
            h = _ffn_after_s5(h, y, g_mix[None], s5_d[j][None], s5_w_glu[j].astype(BF16), ffn_w)
        else:
            o = _ret_mixer(h, g_mix, ret_w_in[j], ret_gn_g[j])
            h = _ffn_after_proj(h, o, ret_w_o[j].astype(BF16), ffn_w)
    return h[:, FRONT:]
```

```python
import functools
import math

import jax
import jax.numpy as jnp
from jax import lax
from jax.experimental import pallas as pl
from jax.experimental.pallas import tpu as pltpu

F32 = jnp.float32
BF16 = jnp.bfloat16

N_META = 16
EPS = 1e-6
NEG_INF = -1e30
ROPE_BASE = 10000.0
MLA_HEADS = 8
MLA_NOPE = 128
MLA_ROPE = 64
MLA_V = 128
MLA_QK = MLA_NOPE + MLA_ROPE
MLA_Q_LORA = 384
MLA_KV_LORA = 256
HGRN_HEADS = 8
HGRN_DK = 128
S5_GROUP = 16
S5_STATE = 64
RET_HEADS = 4

LANES = 128
PAD = LANES - N_META
FRONT = PAD + N_META
ROW_TILE = 384
FFN_ROW_TILE = 704
ATT_BLOCK = 256
ATT_SUBS = 4
ATT_MAX_SHIFT = 48.0
HGRN_CHUNK = 64
HGRN_REF = HGRN_CHUNK // 2 - 1
RET_CHUNK = 128
S5_BLOCK = 16
S5_SLAB_GROUPS = LANES // S5_GROUP
FFN_COLS = 256
VMEM_LIMIT = 56 * 1024 * 1024


def _cparams(*sem):
    return pltpu.CompilerParams(dimension_semantics=sem, vmem_limit_bytes=VMEM_LIMIT)


def _resident(shape):
    nd = len(shape)
    return pl.BlockSpec(shape, lambda *_: (0,) * nd, pipeline_mode=pl.Buffered(1))


def _rms(x, g):
    return x * lax.rsqrt(jnp.mean(x * x, axis=-1, keepdims=True) + EPS) * g


def _dot(a, b):
    return jnp.dot(a, b, preferred_element_type=F32)


def _dot_nt(a, b):
    return lax.dot_general(a, b, (((1,), (1,)), ((), ())), preferred_element_type=F32)


def _dot_tn(a, b):
    return lax.dot_general(a, b, (((0,), (0,)), ((), ())), preferred_element_type=F32)


def _sigmoid(x):
    return 1.0 / (1.0 + jnp.exp(-x))


def _silu(x):
    return x * _sigmoid(x)


def _gelu_tanh(x):
    return 0.5 * x * (1.0 + jnp.tanh(math.sqrt(2.0 / math.pi) * (x + 0.044715 * (x * x * x))))


def _tile_valid(t, tm):
    return t * tm + lax.broadcasted_iota(jnp.int32, (tm, 1), 0) >= PAD


def _ffn_tail(x, t, g_ref, wup_ref, cw_ref, cb_ref, wdn_ref, o_ref, carry_ref, act_ref):
    tm = x.shape[0]
    hidden = wdn_ref.shape[0]
    valid = _tile_valid(t, tm)
    x = jnp.where(valid, x, 0.0)
    a = _rms(x, g_ref[...]).astype(BF16)
    rid = lax.broadcasted_iota(jnp.int32, (tm, 1), 0)

    @pl.when(t == 0)
    def _():
        carry_ref[...] = jnp.zeros_like(carry_ref)

    def conv_slice(col0):
        cols = slice(col0, col0 + FFN_COLS)
        u = _dot(a, wup_ref[:, cols])
        prev = carry_ref[:, cols]
        p0, p1 = prev[6:7], prev[7:8]
        u1 = jnp.where(rid == 0, p1, pltpu.roll(u, 1, 0))
        u2 = jnp.where(rid == 0, p0, jnp.where(rid == 1, p1, pltpu.roll(u, 2, 0)))
        carry_ref[:, cols] = u[tm - 8:tm]
        w = cw_ref[:, cols]
        return w[0:1] * u2 + w[1:2] * u1 + w[2:3] * u + cb_ref[:, cols]

    for c in range(hidden // FFN_COLS):
        gate = conv_slice(c * FFN_COLS)
        val = conv_slice(hidden + c * FFN_COLS)
        act_ref[:, c * FFN_COLS:(c + 1) * FFN_COLS] = (_silu(gate) * val).astype(BF16)
    y = _dot(act_ref[...], wdn_ref[...])
    o_ref[...] = jnp.where(valid, x + y, 0.0)


def _ffn_proj_kernel(h_ref, o_in_ref, wo_ref, g_ref, wup_ref, cw_ref, cb_ref, wdn_ref,
                     o_ref, carry_ref, act_ref):
    t = pl.program_id(1)
    x = h_ref[...] + _dot(o_in_ref[...], wo_ref[...])
    _ffn_tail(x, t, g_ref, wup_ref, cw_ref, cb_ref, wdn_ref, o_ref, carry_ref, act_ref)


def _ffn_s5_kernel(h_ref, y_ref, gmix_ref, d_ref, wglu_ref, g_ref, wup_ref, cw_ref, cb_ref, wdn_ref,
                   o_ref, carry_ref, act_ref):
    t = pl.program_id(1)
    h = h_ref[...]
    d_model = h.shape[1]
    a = _rms(h, gmix_ref[...])
    z = _gelu_tanh(y_ref[...].astype(F32) + d_ref[...] * a).astype(BF16)
    val = _dot(z, wglu_ref[:, :d_model])
    gate = _dot(z, wglu_ref[:, d_model:])
    x = h + val * _sigmoid(gate)
    _ffn_tail(x, t, g_ref, wup_ref, cw_ref, cb_ref, wdn_ref, o_ref, carry_ref, act_ref)


def _ffn_call(kernel, h, lead_inputs, lead_specs, ffn_w):
    b, lp, d = h.shape
    g, wup, cw, cb, wdn = ffn_w
    hidden = wdn.shape[0]
    tm = FFN_ROW_TILE if lp % FFN_ROW_TILE == 0 else ROW_TILE
    row_spec = lambda width: pl.BlockSpec((None, tm, width), lambda i, t: (i, t, 0))
    return pl.pallas_call(
        kernel,
        name=kernel.__name__.strip('_'),
        grid=(b, lp // tm),
        in_specs=[row_spec(d)] + lead_specs(row_spec) + [
            _resident(g.shape), _resident(wup.shape), _resident(cw.shape), _resident(cb.shape),
            _resident(wdn.shape)],
        out_specs=row_spec(d),
        out_shape=jax.ShapeDtypeStruct((b, lp, d), F32),
        scratch_shapes=[pltpu.VMEM((8, 2 * hidden), F32), pltpu.VMEM((tm, hidden), BF16)],
        compiler_params=_cparams("arbitrary", "arbitrary"),
    )(h, *lead_inputs, g, wup, cw, cb, wdn)


def _ffn_after_proj(h, o_in, wo, ffn_w):
    return _ffn_call(_ffn_proj_kernel, h, (o_in, wo),
                     lambda row_spec: [row_spec(o_in.shape[-1]), _resident(wo.shape)], ffn_w)


def _ffn_after_s5(h, y, gmix, dskip, wglu, ffn_w):
    return _ffn_call(_ffn_s5_kernel, h, (y, gmix, dskip, wglu),
                     lambda row_spec: [row_spec(y.shape[-1]), _resident(gmix.shape), _resident(dskip.shape),
                                       _resident(wglu.shape)], ffn_w)


def _rope_swap(r):
    lane = lax.broadcasted_iota(jnp.int32, r.shape, 1)
    half = MLA_ROPE // 2
    return jnp.where(lane < half, pltpu.roll(r, LANES - half, 1), pltpu.roll(r, half, 1))


def _mla_qkv_kernel(h_ref, gmix_ref, wd_ref, gcq_ref, gckv_ref, wuq_ref, wukv_ref,
                    gqn_ref, gqr_ref, gkn_ref, gkr_ref, qone_ref, kshift_ref, cos_ref, sin_ref,
                    q_ref, k_ref, v_ref):
    a = _rms(h_ref[...], gmix_ref[...]).astype(BF16)
    down = _dot(a, wd_ref[...])
    kv_lo = MLA_Q_LORA + MLA_KV_LORA
    cq = _rms(down[:, :MLA_Q_LORA], gcq_ref[...]).astype(BF16)
    ckv = _rms(down[:, MLA_Q_LORA:kv_lo], gckv_ref[...]).astype(BF16)
    kpe = down[:, kv_lo:kv_lo + LANES]
    q = _dot(cq, wuq_ref[...])
    kv = _dot(ckv, wukv_ref[...])
    cos, sin = cos_ref[...], sin_ref[...]

    def rope(r):
        return r * cos + _rope_swap(r) * sin

    hw = 2 * LANES
    pair_ones = (lax.broadcasted_iota(jnp.int32, (hw, hw), 0) // LANES
                 == lax.broadcasted_iota(jnp.int32, (hw, hw), 1) // LANES).astype(BF16)

    def row_sums(x0, x1):
        both = _dot(jnp.concatenate([x0, x1], axis=1).astype(BF16), pair_ones)
        return both[:, :LANES], both[:, LANES:]

    sq_pe = kpe * kpe
    ss_pe, _ = row_sums(sq_pe, sq_pe)
    k_rope = rope(kpe * gkr_ref[...])
    scale = MLA_QK ** -0.5 * math.log2(math.e)
    for h0 in range(0, MLA_HEADS, 2):
        pair = (h0, h0 + 1)
        qn = [q[:, hd * hw:hd * hw + LANES] for hd in pair]
        qr = [q[:, hd * hw + LANES:(hd + 1) * hw] for hd in pair]
        kn = [kv[:, hd * hw:hd * hw + LANES] for hd in pair]
        ss_q = row_sums(qn[0] * qn[0] + qr[0] * qr[0], qn[1] * qn[1] + qr[1] * qr[1])
        ss_k = row_sums(kn[0] * kn[0], kn[1] * kn[1])
        for n, hd in enumerate(pair):
            inv_q = lax.rsqrt(ss_q[n] / MLA_QK + EPS) * scale
            q_ref[:, hd * hw:hd * hw + LANES] = (qn[n] * inv_q * gqn_ref[...]).astype(BF16)
            q_ref[:, hd * hw + LANES:(hd + 1) * hw] = (
                rope(qr[n] * gqr_ref[...]) * inv_q + qone_ref[...]).astype(BF16)
            inv_k = lax.rsqrt((ss_k[n] + ss_pe) / MLA_QK + EPS)
            k_ref[:, hd * hw:hd * hw + LANES] = (kn[n] * inv_k * gkn_ref[...]).astype(BF16)
            k_ref[:, hd * hw + LANES:(hd + 1) * hw] = (k_rope * inv_k + kshift_ref[...]).astype(BF16)
            v_ref[:, hd * LANES:(hd + 1) * LANES] = kv[:, hd * hw + LANES:(hd + 1) * hw].astype(BF16)


def _attn_kernel(q_ref, k_ref, v_ref, o_ref, ve_s):
    lp, dv = v_ref.shape
    blk, subs = ATT_BLOCK, ATT_SUBS
    n_sup = (lp - FRONT) // (blk * subs)
    ve_s[:, 0:dv] = v_ref[...]
    ve_s[:, dv:2 * dv] = jnp.ones((lp, dv), BF16)
    k0, v0 = k_ref[0:FRONT], ve_s[0:FRONT]
    key_ok0 = lax.broadcasted_iota(jnp.int32, (1, FRONT), 1) >= PAD
    chunk = FRONT // 2
    diag_ok = (lax.broadcasted_iota(jnp.int32, (blk, blk), 0) // chunk
               >= lax.broadcasted_iota(jnp.int32, (blk, blk), 1) // chunk)

    def first_block(q):
        s = jnp.where(key_ok0, _dot_nt(q, k0), NEG_INF)
        m = jnp.max(s, axis=-1, keepdims=True)
        return m, _dot(jnp.exp2(s - m).astype(BF16), v0)

    def update(carry, s, vb):
        m, acc = carry
        m_new = jnp.maximum(m, jnp.max(s, axis=-1, keepdims=True))
        p = jnp.exp2(s - m_new).astype(BF16)
        return m_new, jnp.exp2(m - m_new) * acc + _dot(p, vb)

    def finish(carry):
        acc = carry[1]
        return (acc[:, 0:dv] / acc[:, dv:2 * dv]).astype(o_ref.dtype)

    o_ref[0:FRONT] = finish(first_block(q_ref[0:FRONT]))

    def sup_body(i, _):
        base = FRONT + i * (blk * subs)
        rows = [pl.ds(pl.multiple_of(base + a * blk, LANES), blk) for a in range(subs)]

        def key_rows(j):
            return pl.ds(pl.multiple_of(FRONT + j * blk, LANES), blk)

        def scores(j):
            kb = k_ref[key_rows(j)]
            return tuple(_dot_nt(q_ref[rows[a]], kb) for a in range(subs))

        def kv_body(j, state):
            carries, s_cur = state
            s_next = scores(j + 1)
            vb = ve_s[key_rows(j)]
            return tuple(update(carries[a], s_cur[a], vb) for a in range(subs)), s_next

        carries, s_cur = lax.fori_loop(0, i * subs, kv_body,
                                       (tuple(first_block(q_ref[r]) for r in rows), scores(0)))
        carries = list(carries)
        for c in range(subs):
            if c > 0:
                kb = k_ref[rows[c]]
                s_cur = [None] * c + [_dot_nt(q_ref[rows[a]], kb) for a in range(c, subs)]
            vb = ve_s[rows[c]]
            for a in range(c, subs):
                s = jnp.where(diag_ok, s_cur[a], NEG_INF) if a == c else s_cur[a]
                carries[a] = update(carries[a], s, vb)
        for a in range(subs):
            o_ref[rows[a]] = finish(carries[a])
        return 0

    lax.fori_loop(0, n_sup, sup_body, 0)


def _attn_bounded_kernel(q_ref, k_ref, v_ref, o_ref, ve_s):
    lp, dv = v_ref.shape
    blk, subs = ATT_BLOCK, ATT_SUBS
    n_sup = (lp - FRONT) // (blk * subs)
    ve_s[:, 0:dv] = v_ref[...]
    ve_s[:, dv:2 * dv] = jnp.ones((lp, dv), BF16)
    k0, v0 = k_ref[0:FRONT], ve_s[0:FRONT]
    key_ok0 = lax.broadcasted_iota(jnp.int32, (1, FRONT), 1) >= PAD
    chunk = FRONT // 2
    diag_ok = (lax.broadcasted_iota(jnp.int32, (blk, blk), 0) // chunk
               >= lax.broadcasted_iota(jnp.int32, (blk, blk), 1) // chunk)

    def weigh(s, vb, ok=None):
        if ok is not None:
            s = jnp.where(ok, s, NEG_INF)
        return _dot(jnp.exp2(s).astype(BF16), vb)

    def finish(acc):
        return (acc[:, 0:dv] / acc[:, dv:2 * dv]).astype(o_ref.dtype)

    o_ref[0:FRONT] = finish(weigh(_dot_nt(q_ref[0:FRONT], k0), v0, key_ok0))

    def sup_body(i, _):
        base = FRONT + i * (blk * subs)
        rows = [pl.ds(pl.multiple_of(base + a * blk, LANES), blk) for a in range(subs)]

        def key_rows(j):
            return pl.ds(pl.multiple_of(FRONT + j * 2 * blk, LANES), 2 * blk)

        def scores(j, first=0):
            kb = k_ref[key_rows(j)]
            return [_dot_nt(q_ref[rows[a]], kb) for a in range(first, subs)]

        def kv_body(j, state):
            accs, s_cur = state
            s_next = scores(j + 1)
            vb = ve_s[key_rows(j)]
            parts = [weigh(s, vb) for s in s_cur]
            return tuple(acc + part for acc, part in zip(accs, parts)), tuple(s_next)

        s0 = [_dot_nt(q_ref[r], k0) for r in rows]
        n_pairs = i * (subs // 2)
        accs, s_cur = lax.fori_loop(0, n_pairs, kv_body,
                                    (tuple(weigh(s, v0, key_ok0) for s in s0), tuple(scores(0))))
        accs = list(accs)
        q_chunk = lax.broadcasted_iota(jnp.int32, (blk, 2 * blk), 0) // chunk
        k_chunk = lax.broadcasted_iota(jnp.int32, (blk, 2 * blk), 1) // chunk
        for pair in range(subs // 2):
            first = 2 * pair
            if pair > 0:
                s_cur = scores(n_pairs + pair, first=first)
            vb = ve_s[key_rows(n_pairs + pair)]
            parts = []
            for n, s in enumerate(s_cur[-(subs - first):]):
                ok = (k_chunk <= q_chunk + n * (blk // chunk)) if n < 2 else None
                parts.append(weigh(s, vb, ok))
            for n, part in enumerate(parts):
                accs[first + n] = accs[first + n] + part
        for a in range(subs):
            o_ref[rows[a]] = finish(accs[a])
        return 0

    lax.fori_loop(0, n_sup, sup_body, 0)


def _mla_rope_tables(lp):
    half = MLA_ROPE // 2
    pos = jnp.maximum(jnp.arange(lp, dtype=F32) - PAD, 0.0)
    inv_freq = 1.0 / (ROPE_BASE ** (jnp.arange(0, MLA_ROPE, 2, dtype=F32) / MLA_ROPE))
    ang = pos[:, None] * inv_freq[None, :]
    c, s = jnp.cos(ang), jnp.sin(ang)
    z = jnp.zeros((lp, LANES - MLA_ROPE), F32)
    return jnp.concatenate([c, c, z], axis=1), jnp.concatenate([-s, s, z], axis=1)


def _mla_mixer(h, g_mix, w_down, g_cq, g_ckv, w_uq, w_ukv, g_qhead, g_khead):
    b, lp, d = h.shape
    nh, hw = MLA_HEADS, 2 * LANES
    kv_lo = MLA_Q_LORA + MLA_KV_LORA
    wd = jnp.pad(w_down, ((0, 0), (0, kv_lo + LANES - w_down.shape[1]))).astype(BF16)
    wuq = jnp.pad(w_uq.reshape(MLA_Q_LORA, nh, MLA_QK), ((0, 0), (0, 0), (0, hw - MLA_QK)))
    wuq = wuq.reshape(MLA_Q_LORA, nh * hw).astype(BF16)
    wukv = w_ukv.astype(BF16)
    zpad = jnp.zeros((LANES - MLA_ROPE,), F32)
    gqn, gkn = g_qhead[None, :MLA_NOPE], g_khead[None, :MLA_NOPE]
    gqr = jnp.concatenate([g_qhead[MLA_NOPE:], zpad])[None]
    gkr = jnp.concatenate([g_khead[MLA_NOPE:], zpad])[None]
    cos, sin = _mla_rope_tables(lp)
    shift = (math.log2(math.e) * math.sqrt(MLA_QK)) * jnp.max(jnp.abs(g_qhead)) * jnp.max(jnp.abs(g_khead))
    spare = jnp.arange(LANES) == MLA_ROPE
    qone = jnp.where(spare, 1.0, 0.0).astype(F32)[None]
    kshift = jnp.where(spare, -shift, 0.0).astype(F32)[None]
    tm = ROW_TILE
    row_spec = lambda width: pl.BlockSpec((None, tm, width), lambda i, t: (i, t, 0))
    tab_spec = pl.BlockSpec((tm, LANES), lambda i, t: (t, 0))
    small = [g_mix[None], wd, g_cq[None], g_ckv[None], wuq, wukv, gqn, gqr, gkn, gkr, qone, kshift]
    q, k, v = pl.pallas_call(
        _mla_qkv_kernel,
        name='mla_qkv',
        grid=(b, lp // tm),
        in_specs=[row_spec(d)] + [_resident(w.shape) for w in small] + [tab_spec, tab_spec],
        out_specs=[row_spec(nh * hw), row_spec(nh * hw), row_spec(nh * MLA_V)],
        out_shape=[jax.ShapeDtypeStruct((b, lp, nh * hw), BF16), jax.ShapeDtypeStruct((b, lp, nh * hw), BF16),
                   jax.ShapeDtypeStruct((b, lp, nh * MLA_V), BF16)],
        compiler_params=_cparams("parallel", "parallel"),
    )(h, *small, cos, sin)
    head_spec = lambda width: pl.BlockSpec((None, lp, width), lambda i, j: (i, 0, j))

    def attend(body, name):
        return pl.pallas_call(
            body,
            name=name,
            grid=(b, nh),
            in_specs=[head_spec(hw), head_spec(hw), head_spec(MLA_V)],
            out_specs=head_spec(MLA_V),
            out_shape=jax.ShapeDtypeStruct((b, lp, nh * MLA_V), BF16),
            scratch_shapes=[pltpu.VMEM((lp, 2 * MLA_V), BF16)],
            compiler_params=_cparams("parallel", "parallel"),
        )

    return lax.cond(shift <= ATT_MAX_SHIFT,
                    attend(_attn_bounded_kernel, 'mla_attention_bounded'),
                    attend(_attn_kernel, 'mla_attention'), q, k, v)


def _hgrn_kernel(h_ref, gmix_ref, win_ref, lb_ref, go_ref, tri_ref, o_ref,
                 state_ref, q_s, k_s, v_s, lf_s, gate_s):
    t = pl.program_id(1)
    tm, d = h_ref.shape
    dk = HGRN_DK
    c = HGRN_CHUNK

    @pl.when(t == 0)
    def _():
        state_ref[...] = jnp.zeros_like(state_ref)

    a = _rms(h_ref[...], gmix_ref[...]).astype(BF16)
    lb = lb_ref[...]
    q_s[...] = _silu(_dot(a, win_ref[:, 0:d]))
    forget = lb + (1.0 - lb) * _sigmoid(_dot(a, win_ref[:, d:2 * d]))
    lf_s[...] = jnp.log(forget)
    k_s[...] = 1.0 - forget
    v_s[...] = _dot(a, win_ref[:, 2 * d:3 * d]).astype(BF16)
    gate_s[...] = _silu(_dot(a, win_ref[:, 3 * d:4 * d]))
    causal = (lax.broadcasted_iota(jnp.int32, (c, c), 0) >= lax.broadcasted_iota(jnp.int32, (c, c), 1))
    tri = tri_ref[...]

    def chunk_body(ci, _):
        rows = pl.ds(pl.multiple_of(ci * c, c), c)
        lf = lf_s[rows]
        lf_hi = lf.astype(BF16)
        lf_lo = (lf - lf_hi.astype(F32)).astype(BF16)
        cum = _dot(tri, lf_hi) + _dot(tri, lf_lo)
        ref_row = cum[HGRN_REF:HGRN_REF + 1]
        last = cum[c - 1:c]
        q, k, v = q_s[rows], k_s[rows], v_s[rows]
        q_rel = (q * jnp.exp(cum - ref_row)).astype(BF16)
        k_rel = (k * jnp.exp(ref_row - cum)).astype(BF16)
        q_dec = (q * jnp.exp(cum)).astype(BF16)
        k_tail = (k * jnp.exp(last - cum)).astype(BF16)
        decay = jnp.exp(last)
        gate = gate_s[rows]
        heads = [slice(hd * dk, (hd + 1) * dk) for hd in range(d // dk)]
        attn = [jnp.where(causal, _dot_nt(q_rel[:, sl], k_rel[:, sl]), 0.0).astype(BF16) for sl in heads]
        inter = [_dot_nt(q_dec[:, sl], state_ref[hd].astype(BF16)) for hd, sl in enumerate(heads)]
        for hd, sl in enumerate(heads):
            state_ref[hd] = state_ref[hd] * decay[:, sl] + _dot_tn(v[:, sl], k_tail[:, sl])
        for hd, sl in enumerate(heads):
            o = _dot(attn[hd], v[:, sl]) + inter[hd]
            o_ref[rows, sl] = (_rms(o, go_ref[...]) * gate[:, sl]).astype(o_ref.dtype)
        return 0

    lax.fori_loop(0, tm // c, chunk_body, 0, unroll=3)


def _hgrn_mixer(h, g_mix, w_in, lb, g_o):
    b, lp, d = h.shape
    tm = ROW_TILE
    c = HGRN_CHUNK
    tri = (jnp.arange(c)[:, None] >= jnp.arange(c)[None, :]).astype(BF16)
    row_spec = pl.BlockSpec((None, tm, d), lambda i, t: (i, t, 0))
    small = [g_mix[None], w_in.astype(BF16), lb[None], g_o[None], tri]
    return pl.pallas_call(
        _hgrn_kernel,
        name='hgrn2',
        grid=(b, lp // tm),
        in_specs=[row_spec] + [_resident(w.shape) for w in small],
        out_specs=row_spec,
        out_shape=jax.ShapeDtypeStruct((b, lp, d), BF16),
        scratch_shapes=[pltpu.VMEM((d // HGRN_DK, d // HGRN_HEADS, HGRN_DK), F32),
                        pltpu.VMEM((tm, d), F32), pltpu.VMEM((tm, d), F32), pltpu.VMEM((tm, d), BF16),
                        pltpu.VMEM((tm, d), F32), pltpu.VMEM((tm, d), F32)],
        compiler_params=_cparams("arbitrary", "arbitrary"),
    )(h, *small)


def _norm_kernel(h_ref, g_ref, o_ref):
    o_ref[...] = _rms(h_ref[...], g_ref[...]).astype(o_ref.dtype)


def _s5_kernel(a_ref, perm_ref, mw_ref, toe_ref, xm_ref, a1_ref, a2_ref, y_ref, w_s, yt_s):
    lp = a_ref.shape[0]
    t = S5_BLOCK
    nblk = lp // t
    gc = xm_ref.shape[0]
    width = 4 * S5_STATE
    half = width // 2
    steps = jnp.concatenate([a_ref[pl.ds(s, nblk, stride=t), :].astype(BF16) for s in range(t)], axis=1)
    u_all = _dot(steps, perm_ref[...]).astype(BF16)
    for g in range(gc):
        u = u_all[:, g * width:(g + 1) * width]
        w_s[:, g * width:(g + 1) * width] = _dot(u, mw_ref[g])
        yt_s[:, g * width:(g + 1) * width] = _dot(u, toe_ref[g])
    a1, a2 = a1_ref[...], a2_ref[...]

    def swap_halves(z):
        parts = []
        for g in range(gc):
            parts += [z[:, g * width + half:(g + 1) * width], z[:, g * width:g * width + half]]
        return jnp.concatenate(parts, axis=1)

    def step(blk, z):
        row = pl.ds(blk, 1)
        w = w_s[row, :]
        w_s[row, :] = z
        return a1 * z + a2 * swap_halves(z) + w

    lax.fori_loop(0, nblk, step, jnp.zeros((1, gc * width), F32))
    ys = []
    for g in range(gc):
        x_start = w_s[:, g * width:g * width + half].astype(BF16)
        ys.append((yt_s[:, g * width:(g + 1) * width] + _dot(x_start, xm_ref[g])).astype(BF16))
    out = _dot_nt(jnp.concatenate(ys, axis=1), perm_ref[...])
    for s in range(t):
        y_ref[pl.ds(s, nblk, stride=t), :] = out[:, s * LANES:(s + 1) * LANES]


def _s5_operators(lam_re, lam_im, log_dt, b_re, b_im, c_re, c_im):
    hp = lax.Precision.HIGHEST
    t = S5_BLOCK
    dt = jnp.exp(log_dt)[:, None]
    mag = jnp.exp(lam_re * dt)
    abar_re, abar_im = mag * jnp.cos(lam_im * dt), mag * jnp.sin(lam_im * dt)
    den = lam_re * lam_re + lam_im * lam_im
    zoh_re = ((abar_re - 1.0) * lam_re + abar_im * lam_im) / den
    zoh_im = (abar_im * lam_re - (abar_re - 1.0) * lam_im) / den
    bb_re = zoh_re[..., None] * b_re - zoh_im[..., None] * b_im
    bb_im = zoh_re[..., None] * b_im + zoh_im[..., None] * b_re
    steps = jnp.arange(t + 1, dtype=F32)[:, None, None] * dt[None]
    pmag = jnp.exp(lam_re[None] * steps)
    pw_re, pw_im = pmag * jnp.cos(lam_im[None] * steps), pmag * jnp.sin(lam_im[None] * steps)
    cp_re = c_re[None] * pw_re[:t, :, None, :] - c_im[None] * pw_im[:t, :, None, :]
    cp_im = c_re[None] * pw_im[:t, :, None, :] + c_im[None] * pw_re[:t, :, None, :]
    taps = jnp.einsum("tgjp,gpk->gktj", jnp.concatenate([cp_re, -cp_im], axis=-1),
                      jnp.concatenate([bb_re, bb_im], axis=1), precision=hp)
    ar_t = jnp.arange(t)
    place = (ar_t[None, :, None] - ar_t[:, None, None] == ar_t[None, None, :]).astype(BF16)
    toe = jnp.einsum("stu,gkuj->gsktj", place, taps.astype(BF16)).reshape(-1, t * S5_GROUP, t * S5_GROUP)
    rev_re, rev_im = pw_re[t - 1::-1][:t], pw_im[t - 1::-1][:t]
    w_re = rev_re[..., None] * bb_re[None] - rev_im[..., None] * bb_im[None]
    w_im = rev_re[..., None] * bb_im[None] + rev_im[..., None] * bb_re[None]
    to_rows = lambda m: m.transpose(1, 0, 3, 2).reshape(m.shape[1], t * S5_GROUP, S5_STATE)
    w_re, w_im = to_rows(w_re), to_rows(w_im)
    mw = jnp.concatenate([w_re, w_im, w_im, w_re], axis=-1)
    cq_re = c_re[None] * pw_re[1:, :, None, :] - c_im[None] * pw_im[1:, :, None, :]
    cq_im = c_re[None] * pw_im[1:, :, None, :] + c_im[None] * pw_re[1:, :, None, :]
    to_cols = lambda m: m.transpose(1, 3, 0, 2).reshape(m.shape[1], S5_STATE, t * S5_GROUP)
    xm = jnp.concatenate([to_cols(cq_re), -to_cols(cq_im)], axis=1)
    ar, ai = pw_re[t], pw_im[t]
    a1 = jnp.concatenate([ar, ar, ar, ar], axis=-1)
    a2 = jnp.concatenate([-ai, ai, ai, -ai], axis=-1)
    per_slab = lambda m: m.reshape(-1, 1, S5_SLAB_GROUPS * 4 * S5_STATE)
    return mw.astype(BF16), toe, xm.astype(BF16), per_slab(a1), per_slab(a2)


def _s5_permutation():
    t, kk = S5_BLOCK, S5_GROUP
    row = jnp.arange(t * LANES)
    s, lane = row // LANES, row % LANES
    target = (lane // kk) * (t * kk) + s * kk + lane % kk
    return (target[:, None] == row[None, :]).astype(BF16)


def _s5_mixer(h, g_mix, lam_re, lam_im, log_dt, b_re, b_im, c_re, c_im):
    b, lp, d = h.shape
    t, kk = S5_BLOCK, S5_GROUP
    nblk = lp // t
    tm = ROW_TILE
    row_spec = pl.BlockSpec((None, tm, d), lambda i, j: (i, j, 0))
    a = pl.pallas_call(
        _norm_kernel,
        name='s5_norm',
        grid=(b, lp // tm),
        in_specs=[row_spec, _resident((1, d))],
        out_specs=row_spec,
        out_shape=jax.ShapeDtypeStruct((b, lp, d), F32),
        compiler_params=_cparams("parallel", "parallel"),
    )(h, g_mix[None])
    mw, toe, xm, a1, a2 = _s5_operators(lam_re, lam_im, log_dt, b_re, b_im, c_re, c_im)
    sel = _s5_permutation()
    gc = S5_SLAB_GROUPS
    width = 4 * S5_STATE
    slab_spec = pl.BlockSpec((None, lp, LANES), lambda i, j: (j, 0, i))
    grp_spec = lambda r, c: pl.BlockSpec((gc, r, c), lambda i, j: (i, 0, 0))
    coef_spec = pl.BlockSpec((None, 1, gc * width), lambda i, j: (i, 0, 0))
    return pl.pallas_call(
        _s5_kernel,
        name='s5_scan',
        grid=(d // LANES, b),
        in_specs=[slab_spec, _resident(sel.shape), grp_spec(t * kk, width), grp_spec(t * kk, t * kk),
                  grp_spec(2 * S5_STATE, t * kk), coef_spec, coef_spec],
        out_specs=slab_spec,
        out_shape=jax.ShapeDtypeStruct((b, lp, d), F32),
        scratch_shapes=[pltpu.VMEM((nblk, gc * width), F32), pltpu.VMEM((nblk, gc * width), F32)],
        compiler_params=_cparams("parallel", "parallel"),
    )(a, sel, mw, toe, xm, a1, a2)


def _ret_kernel(h_ref, gmix_ref, win_ref, cos_ref, sin_ref, gn_ref, o_ref, state_ref, q_s, k_s, v_s, g_s):
    t = pl.program_id(1)
    tm, d = h_ref.shape
    nh = RET_HEADS
    dk = d // nh
    dv = 2 * dk
    c = RET_CHUNK
    half = dk // 2

    @pl.when(t == 0)
    def _():
        state_ref[...] = jnp.zeros_like(state_ref)

    a = _rms(h_ref[...], gmix_ref[...]).astype(BF16)
    cos, sin = cos_ref[...], sin_ref[...]
    for hd in range(nh):
        for ref, col0, scl in ((q_s, 0, 1.0), (k_s, d, dk ** -0.5)):
            x1 = _dot(a, win_ref[:, col0 + hd * dk:col0 + hd * dk + half])
            x2 = _dot(a, win_ref[:, col0 + hd * dk + half:col0 + (hd + 1) * dk])
            ref[:, hd * dk:hd * dk + half] = (x1 * cos - x2 * sin) * scl
            ref[:, hd * dk + half:(hd + 1) * dk] = (x1 * sin + x2 * cos) * scl
        v_s[:, hd * dv:(hd + 1) * dv] = _dot(a, win_ref[:, 2 * d + hd * dv:2 * d + (hd + 1) * dv]).astype(BF16)
        g_s[:, hd * dv:(hd + 1) * dv] = _silu(_dot(a, win_ref[:, 4 * d + hd * dv:4 * d + (hd + 1) * dv]))
    ri = lax.broadcasted_iota(jnp.int32, (c, c), 0)
    ci_ = lax.broadcasted_iota(jnp.int32, (c, c), 1)
    diff = (ri - ci_).astype(F32)
    pos = lax.broadcasted_iota(jnp.int32, (c, 1), 0).astype(F32)

    def chunk_body(cc, _):
        rows = pl.ds(pl.multiple_of(cc * c, c), c)
        log_g = [math.log(1.0 - 2.0 ** (-5.0 - hd)) for hd in range(nh)]
        scores, inter = [], []
        for hd, lg in enumerate(log_g):
            q = q_s[rows, hd * dk:(hd + 1) * dk]
            k = k_s[rows, hd * dk:(hd + 1) * dk]
            v = v_s[rows, hd * dv:(hd + 1) * dv]
            st = state_ref[hd]
            decay = jnp.where(diff >= 0, jnp.exp(diff * lg), 0.0)
            scores.append((_dot_nt(q.astype(BF16), k.astype(BF16)) * decay).astype(BF16))
            q_in = (q * jnp.exp((pos + 1.0) * lg)).astype(BF16)
            k_out = (k * jnp.exp((c - 1.0 - pos) * lg)).astype(BF16)
            inter.append(_dot(q_in, st.astype(BF16)))
            state_ref[hd] = st * math.exp(c * lg) + _dot_tn(k_out, v)
        for hd in range(nh):
            o = _dot(scores[hd], v_s[rows, hd * dv:(hd + 1) * dv]) + inter[hd]
            mu = jnp.mean(o, axis=-1, keepdims=True)
            oc = o - mu
            var = jnp.mean(oc * oc, axis=-1, keepdims=True)
            o = oc * lax.rsqrt(var + EPS) * gn_ref[:, hd * dv:(hd + 1) * dv] * g_s[rows, hd * dv:(hd + 1) * dv]
            o_ref[rows, hd * dv:(hd + 1) * dv] = o.astype(o_ref.dtype)
        return 0

    lax.fori_loop(0, tm // c, chunk_body, 0, unroll=True)


def _ret_mixer(h, g_mix, w_in, gn_g):
    b, lp, d = h.shape
    nh = RET_HEADS
    dk = d // nh
    tm = ROW_TILE
    pos = jnp.maximum(jnp.arange(lp, dtype=F32) - PAD, 0.0)
    inv_freq = 1.0 / (ROPE_BASE ** (jnp.arange(0, dk, 2, dtype=F32) / dk))
    ang = pos[:, None] * inv_freq[None, :]
    cos, sin = jnp.cos(ang), jnp.sin(ang)
    row_spec = lambda width: pl.BlockSpec((None, tm, width), lambda i, t: (i, t, 0))
    tab_spec = pl.BlockSpec((tm, dk // 2), lambda i, t: (t, 0))
    win = w_in.astype(BF16)
    return pl.pallas_call(
        _ret_kernel,
        name='retention',
        grid=(b, lp // tm),
        in_specs=[row_spec(d), _resident((1, d)), _resident(win.shape), tab_spec, tab_spec,
                  _resident((1, 2 * d))],
        out_specs=row_spec(2 * d),
        out_shape=jax.ShapeDtypeStruct((b, lp, 2 * d), BF16),
        scratch_shapes=[pltpu.VMEM((nh, dk, 2 * dk), F32), pltpu.VMEM((tm, d), F32), pltpu.VMEM((tm, d), F32),
                        pltpu.VMEM((tm, 2 * d), BF16), pltpu.VMEM((tm, 2 * d), F32)],
        compiler_params=_cparams("arbitrary", "arbitrary"),
    )(h, g_mix[None], win, cos, sin, gn_g[None])


def kernel(x, meta_tokens, norm_mix_g, norm_ffn_g, mla_w_down, mla_cq_norm_g, mla_ckv_norm_g, mla_w_uq, mla_w_ukv, mla_q_head_g, mla_k_head_g, mla_w_o, hgrn_w_in, hgrn_lb_logits, hgrn_o_norm_g, hgrn_w_o, s5_lam_re, s5_lam_im, s5_log_dt, s5_b_re, s5_b_im, s5_c_re, s5_c_im, s5_d, s5_w_glu, ret_w_in, ret_gn_g, ret_w_o, ffn_w_up, ffn_conv_w, ffn_conv_b, ffn_w_down):
    b, seq, d = x.shape
    depth = norm_mix_g.shape[0]
    h = jnp.concatenate([jnp.zeros((b, PAD, d), x.dtype),
                         jnp.broadcast_to(meta_tokens[None].astype(x.dtype), (b, N_META, d)), x], axis=1)
    lb_cum = jnp.cumsum(jax.nn.softmax(hgrn_lb_logits.astype(F32), axis=0), axis=0)
    lb_all = lb_cum - lb_cum[0:1]
    for i in range(depth):
        m, j = i % 4, i // 4
        ffn_w = (norm_ffn_g[i][None], ffn_w_up[i].astype(BF16), ffn_conv_w[i], ffn_conv_b[i][None],
                 ffn_w_down[i].astype(BF16))
        g_mix = norm_mix_g[i]
        if m == 0:
            o = _mla_mixer(h, g_mix, mla_w_down[j], mla_cq_norm_g[j], mla_ckv_norm_g[j], mla_w_uq[j],
                           mla_w_ukv[j], mla_q_head_g[j], mla_k_head_g[j])
            h = _ffn_after_proj(h, o, mla_w_o[j].astype(BF16), ffn_w)
        elif m == 1:
            o = _hgrn_mixer(h, g_mix, hgrn_w_in[j], lb_all[i], hgrn_o_norm_g[j])
            h = _ffn_after_proj(h, o, hgrn_w_o[j].astype(BF16), ffn_w)
        elif m == 2:
            y = _s5_mixer(h, g_mix, s5_lam_re[j], s5_lam_im[j], s5_log_dt[j], s5_b_re[j], s5_b_im[j],
                          s5_c_re[j], s5_c_im[j])
            h = _ffn_after_s5(h, y, g_mix[None], s5_d[j][None], s5_w_glu[j].astype(BF16), ffn_w)
        else:
            o = _ret_mixer(h, g_mix, ret_w_in[j], ret_gn_g[j])
            h = _ffn_after_proj(h, o, ret_w_o[j].astype(BF16), ffn_w)
    return h[:, FRONT:]
```

```python
import functools
import math

import jax
import jax.numpy as jnp
from jax import lax
from jax.experimental import pallas as pl
from jax.experimental.pallas import tpu as pltpu

F32 = jnp.float32
BF16 = jnp.bfloat16

N_META = 16
EPS = 1e-6
NEG_INF = -1e30
ROPE_BASE = 10000.0
MLA_HEADS = 8
MLA_NOPE = 128
MLA_ROPE = 64
MLA_V = 128
MLA_QK = MLA_NOPE + MLA_ROPE
MLA_Q_LORA = 384
MLA_KV_LORA = 256
HGRN_HEADS = 8
HGRN_DK = 128
S5_GROUP = 16
S5_STATE = 64
RET_HEADS = 4

LANES = 128
PAD = LANES - N_META
FRONT = PAD + N_META
ROW_TILE = 384
FFN_ROW_TILE = 704
ATT_BLOCK = 256
ATT_SUBS = 4
ATT_MAX_SHIFT = 48.0
HGRN_CHUNK = 64
HGRN_REF = HGRN_CHUNK // 2 - 1
RET_CHUNK = 128
S5_BLOCK = 16
S5_SLAB_GROUPS = LANES // S5_GROUP
S5_BATCH_ROWS = 2
FFN_COLS = 256
VMEM_LIMIT = 56 * 1024 * 1024


def _cparams(*sem):
    return pltpu.CompilerParams(dimension_semantics=sem, vmem_limit_bytes=VMEM_LIMIT)


def _resident(shape):
    nd = len(shape)
    return pl.BlockSpec(shape, lambda *_: (0,) * nd, pipeline_mode=pl.Buffered(1))


def _rms(x, g):
    return x * lax.rsqrt(jnp.mean(x * x, axis=-1, keepdims=True) + EPS) * g


def _dot(a, b):
    return jnp.dot(a, b, preferred_element_type=F32)


def _dot_nt(a, b):
    return lax.dot_general(a, b, (((1,), (1,)), ((), ())), preferred_element_type=F32)


def _dot_tn(a, b):
    return lax.dot_general(a, b, (((0,), (0,)), ((), ())), preferred_element_type=F32)


def _sigmoid(x):
    return 1.0 / (1.0 + jnp.exp(-x))


def _silu(x):
    return x * _sigmoid(x)


def _gelu_tanh(x):
    return 0.5 * x * (1.0 + jnp.tanh(math.sqrt(2.0 / math.pi) * (x + 0.044715 * (x * x * x))))


def _tile_valid(t, tm):
    return t * tm + lax.broadcasted_iota(jnp.int32, (tm, 1), 0) >= PAD


def _ffn_tail(x, t, g_ref, wup_ref, cw_ref, cb_ref, wdn_ref, o_ref, carry_ref, act_ref):
    tm = x.shape[0]
    hidden = wdn_ref.shape[0]
    valid = _tile_valid(t, tm)
    x = jnp.where(valid, x, 0.0)
    a = _rms(x, g_ref[...]).astype(BF16)
    rid = lax.broadcasted_iota(jnp.int32, (tm, 1), 0)

    @pl.when(t == 0)
    def _():
        carry_ref[...] = jnp.zeros_like(carry_ref)

    def conv_slice(col0):
        cols = slice(col0, col0 + FFN_COLS)
        u = _dot(a, wup_ref[:, cols])
        prev = carry_ref[:, cols]
        p0, p1 = prev[6:7], prev[7:8]
        u1 = jnp.where(rid == 0, p1, pltpu.roll(u, 1, 0))
        u2 = jnp.where(rid == 0, p0, jnp.where(rid == 1, p1, pltpu.roll(u, 2, 0)))
        carry_ref[:, cols] = u[tm - 8:tm]
        w = cw_ref[:, cols]
        return w[0:1] * u2 + w[1:2] * u1 + w[2:3] * u + cb_ref[:, cols]

    for c in range(hidden // FFN_COLS):
        gate = conv_slice(c * FFN_COLS)
        val = conv_slice(hidden + c * FFN_COLS)
        act_ref[:, c * FFN_COLS:(c + 1) * FFN_COLS] = (_silu(gate) * val).astype(BF16)
    y = _dot(act_ref[...], wdn_ref[...])
    o_ref[...] = jnp.where(valid, x + y, 0.0)


def _ffn_proj_kernel(h_ref, o_in_ref, wo_ref, g_ref, wup_ref, cw_ref, cb_ref, wdn_ref,
                     o_ref, carry_ref, act_ref):
    t = pl.program_id(1)
    x = h_ref[...] + _dot(o_in_ref[...], wo_ref[...])
    _ffn_tail(x, t, g_ref, wup_ref, cw_ref, cb_ref, wdn_ref, o_ref, carry_ref, act_ref)


def _ffn_s5_kernel(h_ref, y_ref, gmix_ref, d_ref, wglu_ref, g_ref, wup_ref, cw_ref, cb_ref, wdn_ref,
                   o_ref, carry_ref, act_ref):
    t = pl.program_id(1)
    h = h_ref[...]
    d_model = h.shape[1]
    a = _rms(h, gmix_ref[...])
    z = _gelu_tanh(y_ref[...].astype(F32) + d_ref[...] * a).astype(BF16)
    val = _dot(z, wglu_ref[:, :d_model])
    gate = _dot(z, wglu_ref[:, d_model:])
    x = h + val * _sigmoid(gate)
    _ffn_tail(x, t, g_ref, wup_ref, cw_ref, cb_ref, wdn_ref, o_ref, carry_ref, act_ref)


def _ffn_call(kernel, h, lead_inputs, lead_specs, ffn_w):
    b, lp, d = h.shape
    g, wup, cw, cb, wdn = ffn_w
    hidden = wdn.shape[0]
    tm = FFN_ROW_TILE if lp % FFN_ROW_TILE == 0 else ROW_TILE
    row_spec = lambda width: pl.BlockSpec((None, tm, width), lambda i, t: (i, t, 0))
    return pl.pallas_call(
        kernel,
        name=kernel.__name__.strip('_'),
        grid=(b, lp // tm),
        in_specs=[row_spec(d)] + lead_specs(row_spec) + [
            _resident(g.shape), _resident(wup.shape), _resident(cw.shape), _resident(cb.shape),
            _resident(wdn.shape)],
        out_specs=row_spec(d),
        out_shape=jax.ShapeDtypeStruct((b, lp, d), F32),
        scratch_shapes=[pltpu.VMEM((8, 2 * hidden), F32), pltpu.VMEM((tm, hidden), BF16)],
        compiler_params=_cparams("arbitrary", "arbitrary"),
    )(h, *lead_inputs, g, wup, cw, cb, wdn)


def _ffn_after_proj(h, o_in, wo, ffn_w):
    return _ffn_call(_ffn_proj_kernel, h, (o_in, wo),
                     lambda row_spec: [row_spec(o_in.shape[-1]), _resident(wo.shape)], ffn_w)


def _ffn_after_s5(h, y, gmix, dskip, wglu, ffn_w):
    return _ffn_call(_ffn_s5_kernel, h, (y, gmix, dskip, wglu),
                     lambda row_spec: [row_spec(y.shape[-1]), _resident(gmix.shape), _resident(dskip.shape),
                                       _resident(wglu.shape)], ffn_w)


def _rope_swap(r):
    lane = lax.broadcasted_iota(jnp.int32, r.shape, 1)
    half = MLA_ROPE // 2
    return jnp.where(lane < half, pltpu.roll(r, LANES - half, 1), pltpu.roll(r, half, 1))


def _mla_qkv_kernel(h_ref, gmix_ref, wd_ref, gcq_ref, gckv_ref, wuq_ref, wukv_ref,
                    gqn_ref, gqr_ref, gkn_ref, gkr_ref, qone_ref, kshift_ref, cos_ref, sin_ref,
                    q_ref, k_ref, v_ref):
    a = _rms(h_ref[...], gmix_ref[...]).astype(BF16)
    down = _dot(a, wd_ref[...])
    kv_lo = MLA_Q_LORA + MLA_KV_LORA
    cq = _rms(down[:, :MLA_Q_LORA], gcq_ref[...]).astype(BF16)
    ckv = _rms(down[:, MLA_Q_LORA:kv_lo], gckv_ref[...]).astype(BF16)
    kpe = down[:, kv_lo:kv_lo + LANES]
    q = _dot(cq, wuq_ref[...])
    kv = _dot(ckv, wukv_ref[...])
    cos, sin = cos_ref[...], sin_ref[...]

    def rope(r):
        return r * cos + _rope_swap(r) * sin

    hw = 2 * LANES
    pair_ones = (lax.broadcasted_iota(jnp.int32, (hw, hw), 0) // LANES
                 == lax.broadcasted_iota(jnp.int32, (hw, hw), 1) // LANES).astype(BF16)

    def row_sums(x0, x1):
        both = _dot(jnp.concatenate([x0, x1], axis=1).astype(BF16), pair_ones)
        return both[:, :LANES], both[:, LANES:]

    sq_pe = kpe * kpe
    ss_pe, _ = row_sums(sq_pe, sq_pe)
    k_rope = rope(kpe * gkr_ref[...])
    scale = MLA_QK ** -0.5 * math.log2(math.e)
    for h0 in range(0, MLA_HEADS, 2):
        pair = (h0, h0 + 1)
        qn = [q[:, hd * hw:hd * hw + LANES] for hd in pair]
        qr = [q[:, hd * hw + LANES:(hd + 1) * hw] for hd in pair]
        kn = [kv[:, hd * hw:hd * hw + LANES] for hd in pair]
        ss_q = row_sums(qn[0] * qn[0] + qr[0] * qr[0], qn[1] * qn[1] + qr[1] * qr[1])
        ss_k = row_sums(kn[0] * kn[0], kn[1] * kn[1])
        for n, hd in enumerate(pair):
            inv_q = lax.rsqrt(ss_q[n] / MLA_QK + EPS) * scale
            q_ref[:, hd * hw:hd * hw + LANES] = (qn[n] * inv_q * gqn_ref[...]).astype(BF16)
            q_ref[:, hd * hw + LANES:(hd + 1) * hw] = (
                rope(qr[n] * gqr_ref[...]) * inv_q + qone_ref[...]).astype(BF16)
            inv_k = lax.rsqrt((ss_k[n] + ss_pe) / MLA_QK + EPS)
            k_ref[:, hd * hw:hd * hw + LANES] = (kn[n] * inv_k * gkn_ref[...]).astype(BF16)
            k_ref[:, hd * hw + LANES:(hd + 1) * hw] = (k_rope * inv_k + kshift_ref[...]).astype(BF16)
            v_ref[:, hd * LANES:(hd + 1) * LANES] = kv[:, hd * hw + LANES:(hd + 1) * hw].astype(BF16)


def _attn_kernel(q_ref, k_ref, v_ref, o_ref, ve_s):
    lp, dv = v_ref.shape
    blk, subs = ATT_BLOCK, ATT_SUBS
    n_sup = (lp - FRONT) // (blk * subs)
    ve_s[:, 0:dv] = v_ref[...]
    ve_s[:, dv:2 * dv] = jnp.ones((lp, dv), BF16)
    k0, v0 = k_ref[0:FRONT], ve_s[0:FRONT]
    key_ok0 = lax.broadcasted_iota(jnp.int32, (1, FRONT), 1) >= PAD
    chunk = FRONT // 2
    diag_ok = (lax.broadcasted_iota(jnp.int32, (blk, blk), 0) // chunk
               >= lax.broadcasted_iota(jnp.int32, (blk, blk), 1) // chunk)

    def first_block(q):
        s = jnp.where(key_ok0, _dot_nt(q, k0), NEG_INF)
        m = jnp.max(s, axis=-1, keepdims=True)
        return m, _dot(jnp.exp2(s - m).astype(BF16), v0)

    def update(carry, s, vb):
        m, acc = carry
        m_new = jnp.maximum(m, jnp.max(s, axis=-1, keepdims=True))
        p = jnp.exp2(s - m_new).astype(BF16)
        return m_new, jnp.exp2(m - m_new) * acc + _dot(p, vb)

    def finish(carry):
        acc = carry[1]
        return (acc[:, 0:dv] / acc[:, dv:2 * dv]).astype(o_ref.dtype)

    o_ref[0:FRONT] = finish(first_block(q_ref[0:FRONT]))

    def sup_body(i, _):
        base = FRONT + i * (blk * subs)
        rows = [pl.ds(pl.multiple_of(base + a * blk, LANES), blk) for a in range(subs)]

        def key_rows(j):
            return pl.ds(pl.multiple_of(FRONT + j * blk, LANES), blk)

        def scores(j):
            kb = k_ref[key_rows(j)]
            return tuple(_dot_nt(q_ref[rows[a]], kb) for a in range(subs))

        def kv_body(j, state):
            carries, s_cur = state
            s_next = scores(j + 1)
            vb = ve_s[key_rows(j)]
            return tuple(update(carries[a], s_cur[a], vb) for a in range(subs)), s_next

        carries, s_cur = lax.fori_loop(0, i * subs, kv_body,
                                       (tuple(first_block(q_ref[r]) for r in rows), scores(0)))
        carries = list(carries)
        for c in range(subs):
            if c > 0:
                kb = k_ref[rows[c]]
                s_cur = [None] * c + [_dot_nt(q_ref[rows[a]], kb) for a in range(c, subs)]
            vb = ve_s[rows[c]]
            for a in range(c, subs):
                s = jnp.where(diag_ok, s_cur[a], NEG_INF) if a == c else s_cur[a]
                carries[a] = update(carries[a], s, vb)
        for a in range(subs):
            o_ref[rows[a]] = finish(carries[a])
        return 0

    lax.fori_loop(0, n_sup, sup_body, 0)


def _attn_bounded_kernel(q_ref, k_ref, v_ref, o_ref, ve_s):
    lp, dv = v_ref.shape
    blk, subs = ATT_BLOCK, ATT_SUBS
    n_sup = (lp - FRONT) // (blk * subs)
    ve_s[:, 0:dv] = v_ref[...]
    ve_s[:, dv:2 * dv] = jnp.ones((lp, dv), BF16)
    k0, v0 = k_ref[0:FRONT], ve_s[0:FRONT]
    key_ok0 = lax.broadcasted_iota(jnp.int32, (1, FRONT), 1) >= PAD
    chunk = FRONT // 2
    diag_ok = (lax.broadcasted_iota(jnp.int32, (blk, blk), 0) // chunk
               >= lax.broadcasted_iota(jnp.int32, (blk, blk), 1) // chunk)

    def weigh(s, vb, ok=None):
        if ok is not None:
            s = jnp.where(ok, s, NEG_INF)
        return _dot(jnp.exp2(s).astype(BF16), vb)

    def finish(acc):
        return (acc[:, 0:dv] / acc[:, dv:2 * dv]).astype(o_ref.dtype)

    o_ref[0:FRONT] = finish(weigh(_dot_nt(q_ref[0:FRONT], k0), v0, key_ok0))

    def sup_body(i, _):
        base = FRONT + i * (blk * subs)
        rows = [pl.ds(pl.multiple_of(base + a * blk, LANES), blk) for a in range(subs)]

        def key_rows(j):
            return pl.ds(pl.multiple_of(FRONT + j * 2 * blk, LANES), 2 * blk)

        def scores(j, first=0):
            kb = k_ref[key_rows(j)]
            return [_dot_nt(q_ref[rows[a]], kb) for a in range(first, subs)]

        def kv_body(j, state):
            accs, s_cur = state
            s_next = scores(j + 1)
            vb = ve_s[key_rows(j)]
            parts = [weigh(s, vb) for s in s_cur]
            return tuple(acc + part for acc, part in zip(accs, parts)), tuple(s_next)

        s0 = [_dot_nt(q_ref[r], k0) for r in rows]
        n_pairs = i * (subs // 2)
        accs, s_cur = lax.fori_loop(0, n_pairs, kv_body,
                                    (tuple(weigh(s, v0, key_ok0) for s in s0), tuple(scores(0))))
        accs = list(accs)
        q_chunk = lax.broadcasted_iota(jnp.int32, (blk, 2 * blk), 0) // chunk
        k_chunk = lax.broadcasted_iota(jnp.int32, (blk, 2 * blk), 1) // chunk
        for pair in range(subs // 2):
            first = 2 * pair
            if pair > 0:
                s_cur = scores(n_pairs + pair, first=first)
            vb = ve_s[key_rows(n_pairs + pair)]
            parts = []
            for n, s in enumerate(s_cur[-(subs - first):]):
                ok = (k_chunk <= q_chunk + n * (blk // chunk)) if n < 2 else None
                parts.append(weigh(s, vb, ok))
            for n, part in enumerate(parts):
                accs[first + n] = accs[first + n] + part
        for a in range(subs):
            o_ref[rows[a]] = finish(accs[a])
        return 0

    lax.fori_loop(0, n_sup, sup_body, 0)


def _mla_rope_tables(lp):
    half = MLA_ROPE // 2
    pos = jnp.maximum(jnp.arange(lp, dtype=F32) - PAD, 0.0)
    inv_freq = 1.0 / (ROPE_BASE ** (jnp.arange(0, MLA_ROPE, 2, dtype=F32) / MLA_ROPE))
    ang = pos[:, None] * inv_freq[None, :]
    c, s = jnp.cos(ang), jnp.sin(ang)
    z = jnp.zeros((lp, LANES - MLA_ROPE), F32)
    return jnp.concatenate([c, c, z], axis=1), jnp.concatenate([-s, s, z], axis=1)


def _mla_mixer(h, g_mix, w_down, g_cq, g_ckv, w_uq, w_ukv, g_qhead, g_khead):
    b, lp, d = h.shape
    nh, hw = MLA_HEADS, 2 * LANES
    kv_lo = MLA_Q_LORA + MLA_KV_LORA
    wd = jnp.pad(w_down, ((0, 0), (0, kv_lo + LANES - w_down.shape[1]))).astype(BF16)
    wuq = jnp.pad(w_uq.reshape(MLA_Q_LORA, nh, MLA_QK), ((0, 0), (0, 0), (0, hw - MLA_QK)))
    wuq = wuq.reshape(MLA_Q_LORA, nh * hw).astype(BF16)
    wukv = w_ukv.astype(BF16)
    zpad = jnp.zeros((LANES - MLA_ROPE,), F32)
    gqn, gkn = g_qhead[None, :MLA_NOPE], g_khead[None, :MLA_NOPE]
    gqr = jnp.concatenate([g_qhead[MLA_NOPE:], zpad])[None]
    gkr = jnp.concatenate([g_khead[MLA_NOPE:], zpad])[None]
    cos, sin = _mla_rope_tables(lp)
    shift = (math.log2(math.e) * math.sqrt(MLA_QK)) * jnp.max(jnp.abs(g_qhead)) * jnp.max(jnp.abs(g_khead))
    spare = jnp.arange(LANES) == MLA_ROPE
    qone = jnp.where(spare, 1.0, 0.0).astype(F32)[None]
    kshift = jnp.where(spare, -shift, 0.0).astype(F32)[None]
    tm = ROW_TILE
    row_spec = lambda width: pl.BlockSpec((None, tm, width), lambda i, t: (i, t, 0))
    tab_spec = pl.BlockSpec((tm, LANES), lambda i, t: (t, 0))
    small = [g_mix[None], wd, g_cq[None], g_ckv[None], wuq, wukv, gqn, gqr, gkn, gkr, qone, kshift]
    q, k, v = pl.pallas_call(
        _mla_qkv_kernel,
        name='mla_qkv',
        grid=(b, lp // tm),
        in_specs=[row_spec(d)] + [_resident(w.shape) for w in small] + [tab_spec, tab_spec],
        out_specs=[row_spec(nh * hw), row_spec(nh * hw), row_spec(nh * MLA_V)],
        out_shape=[jax.ShapeDtypeStruct((b, lp, nh * hw), BF16), jax.ShapeDtypeStruct((b, lp, nh * hw), BF16),
                   jax.ShapeDtypeStruct((b, lp, nh * MLA_V), BF16)],
        compiler_params=_cparams("parallel", "parallel"),
    )(h, *small, cos, sin)
    head_spec = lambda width: pl.BlockSpec((None, lp, width), lambda i, j: (i, 0, j))

    def attend(body, name):
        return pl.pallas_call(
            body,
            name=name,
            grid=(b, nh),
            in_specs=[head_spec(hw), head_spec(hw), head_spec(MLA_V)],
            out_specs=head_spec(MLA_V),
            out_shape=jax.ShapeDtypeStruct((b, lp, nh * MLA_V), BF16),
            scratch_shapes=[pltpu.VMEM((lp, 2 * MLA_V), BF16)],
            compiler_params=_cparams("parallel", "parallel"),
        )

    return lax.cond(shift <= ATT_MAX_SHIFT,
                    attend(_attn_bounded_kernel, 'mla_attention_bounded'),
                    attend(_attn_kernel, 'mla_attention'), q, k, v)


def _hgrn_kernel(h_ref, gmix_ref, win_ref, lb_ref, go_ref, tri_ref, o_ref,
                 state_ref, q_s, k_s, v_s, lf_s, gate_s):
    t = pl.program_id(1)
    tm, d = h_ref.shape
    dk = HGRN_DK
    c = HGRN_CHUNK

    @pl.when(t == 0)
    def _():
        state_ref[...] = jnp.zeros_like(state_ref)

    a = _rms(h_ref[...], gmix_ref[...]).astype(BF16)
    lb = lb_ref[...]
    q_s[...] = _silu(_dot(a, win_ref[:, 0:d]))
    forget = lb + (1.0 - lb) * _sigmoid(_dot(a, win_ref[:, d:2 * d]))
    lf_s[...] = jnp.log(forget)
    k_s[...] = 1.0 - forget
    v_s[...] = _dot(a, win_ref[:, 2 * d:3 * d]).astype(BF16)
    gate_s[...] = _silu(_dot(a, win_ref[:, 3 * d:4 * d]))
    causal = (lax.broadcasted_iota(jnp.int32, (c, c), 0) >= lax.broadcasted_iota(jnp.int32, (c, c), 1))
    tri = tri_ref[...]

    def chunk_body(ci, _):
        rows = pl.ds(pl.multiple_of(ci * c, c), c)
        lf = lf_s[rows]
        lf_hi = lf.astype(BF16)
        lf_lo = (lf - lf_hi.astype(F32)).astype(BF16)
        cum = _dot(tri, lf_hi) + _dot(tri, lf_lo)
        ref_row = cum[HGRN_REF:HGRN_REF + 1]
        last = cum[c - 1:c]
        q, k, v = q_s[rows], k_s[rows], v_s[rows]
        q_rel = (q * jnp.exp(cum - ref_row)).astype(BF16)
        k_rel = (k * jnp.exp(ref_row - cum)).astype(BF16)
        q_dec = (q * jnp.exp(cum)).astype(BF16)
        k_tail = (k * jnp.exp(last - cum)).astype(BF16)
        decay = jnp.exp(last)
        gate = gate_s[rows]
        heads = [slice(hd * dk, (hd + 1) * dk) for hd in range(d // dk)]
        attn = [jnp.where(causal, _dot_nt(q_rel[:, sl], k_rel[:, sl]), 0.0).astype(BF16) for sl in heads]
        inter = [_dot_nt(q_dec[:, sl], state_ref[hd].astype(BF16)) for hd, sl in enumerate(heads)]
        for hd, sl in enumerate(heads):
            state_ref[hd] = state_ref[hd] * decay[:, sl] + _dot_tn(v[:, sl], k_tail[:, sl])
        for hd, sl in enumerate(heads):
            o = _dot(attn[hd], v[:, sl]) + inter[hd]
            o_ref[rows, sl] = (_rms(o, go_ref[...]) * gate[:, sl]).astype(o_ref.dtype)
        return 0

    lax.fori_loop(0, tm // c, chunk_body, 0, unroll=3)


def _hgrn_mixer(h, g_mix, w_in, lb, g_o):
    b, lp, d = h.shape
    tm = ROW_TILE
    c = HGRN_CHUNK
    tri = (jnp.arange(c)[:, None] >= jnp.arange(c)[None, :]).astype(BF16)
    row_spec = pl.BlockSpec((None, tm, d), lambda i, t: (i, t, 0))
    small = [g_mix[None], w_in.astype(BF16), lb[None], g_o[None], tri]
    return pl.pallas_call(
        _hgrn_kernel,
        name='hgrn2',
        grid=(b, lp // tm),
        in_specs=[row_spec] + [_resident(w.shape) for w in small],
        out_specs=row_spec,
        out_shape=jax.ShapeDtypeStruct((b, lp, d), BF16),
        scratch_shapes=[pltpu.VMEM((d // HGRN_DK, d // HGRN_HEADS, HGRN_DK), F32),
                        pltpu.VMEM((tm, d), F32), pltpu.VMEM((tm, d), F32), pltpu.VMEM((tm, d), BF16),
                        pltpu.VMEM((tm, d), F32), pltpu.VMEM((tm, d), F32)],
        compiler_params=_cparams("arbitrary", "arbitrary"),
    )(h, *small)


def _norm_kernel(h_ref, g_ref, o_ref):
    o_ref[...] = _rms(h_ref[...], g_ref[...]).astype(o_ref.dtype)


def _s5_kernel(a_ref, perm_ref, mw_ref, toe_ref, xm_ref, a1_ref, a2_ref, y_ref, w_s, yt_s):
    nbs, lp, _ = a_ref.shape
    t = S5_BLOCK
    nblk = lp // t
    gc = xm_ref.shape[0]
    width = 4 * S5_STATE
    half = width // 2
    steps = jnp.concatenate(
        [jnp.concatenate([a_ref[bb, pl.ds(s, nblk, stride=t), :] for s in range(t)], axis=1) for bb in range(nbs)],
        axis=0).astype(BF16)
    u_all = _dot(steps, perm_ref[...]).astype(BF16)
    for g in range(gc):
        u = u_all[:, g * width:(g + 1) * width]
        w_s[:, g * width:(g + 1) * width] = _dot(u, mw_ref[g])
        yt_s[:, g * width:(g + 1) * width] = _dot(u, toe_ref[g])
    a1, a2 = a1_ref[...], a2_ref[...]

    def swap_halves(z):
        parts = []
        for g in range(gc):
            parts += [z[:, g * width + half:(g + 1) * width], z[:, g * width:g * width + half]]
        return jnp.concatenate(parts, axis=1)

    def step(blk, zs):
        nxt = []
        for bb, z in enumerate(zs):
            row = pl.ds(bb * nblk + blk, 1)
            w = w_s[row, :]
            w_s[row, :] = z
            nxt.append(a1 * z + a2 * swap_halves(z) + w)
        return tuple(nxt)

    lax.fori_loop(0, nblk, step, tuple(jnp.zeros((1, gc * width), F32) for _ in range(nbs)))
    ys = []
    for g in range(gc):
        x_start = w_s[:, g * width:g * width + half].astype(BF16)
        ys.append((yt_s[:, g * width:(g + 1) * width] + _dot(x_start, xm_ref[g])).astype(BF16))
    out = _dot_nt(jnp.concatenate(ys, axis=1), perm_ref[...])
    for bb in range(nbs):
        for s in range(t):
            y_ref[bb, pl.ds(s, nblk, stride=t), :] = out[bb * nblk:(bb + 1) * nblk, s * LANES:(s + 1) * LANES]


def _s5_operators(lam_re, lam_im, log_dt, b_re, b_im, c_re, c_im):
    hp = lax.Precision.HIGHEST
    t = S5_BLOCK
    dt = jnp.exp(log_dt)[:, None]
    mag = jnp.exp(lam_re * dt)
    abar_re, abar_im = mag * jnp.cos(lam_im * dt), mag * jnp.sin(lam_im * dt)
    den = lam_re * lam_re + lam_im * lam_im
    zoh_re = ((abar_re - 1.0) * lam_re + abar_im * lam_im) / den
    zoh_im = (abar_im * lam_re - (abar_re - 1.0) * lam_im) / den
    bb_re = zoh_re[..., None] * b_re - zoh_im[..., None] * b_im
    bb_im = zoh_re[..., None] * b_im + zoh_im[..., None] * b_re
    steps = jnp.arange(t + 1, dtype=F32)[:, None, None] * dt[None]
    pmag = jnp.exp(lam_re[None] * steps)
    pw_re, pw_im = pmag * jnp.cos(lam_im[None] * steps), pmag * jnp.sin(lam_im[None] * steps)
    cp_re = c_re[None] * pw_re[:t, :, None, :] - c_im[None] * pw_im[:t, :, None, :]
    cp_im = c_re[None] * pw_im[:t, :, None, :] + c_im[None] * pw_re[:t, :, None, :]
    taps = jnp.einsum("tgjp,gpk->gktj", jnp.concatenate([cp_re, -cp_im], axis=-1),
                      jnp.concatenate([bb_re, bb_im], axis=1), precision=hp)
    ar_t = jnp.arange(t)
    place = (ar_t[None, :, None] - ar_t[:, None, None] == ar_t[None, None, :]).astype(BF16)
    toe = jnp.einsum("stu,gkuj->gsktj", place, taps.astype(BF16)).reshape(-1, t * S5_GROUP, t * S5_GROUP)
    rev_re, rev_im = pw_re[t - 1::-1][:t], pw_im[t - 1::-1][:t]
    w_re = rev_re[..., None] * bb_re[None] - rev_im[..., None] * bb_im[None]
    w_im = rev_re[..., None] * bb_im[None] + rev_im[..., None] * bb_re[None]
    to_rows = lambda m: m.transpose(1, 0, 3, 2).reshape(m.shape[1], t * S5_GROUP, S5_STATE)
    w_re, w_im = to_rows(w_re), to_rows(w_im)
    mw = jnp.concatenate([w_re, w_im, w_im, w_re], axis=-1)
    cq_re = c_re[None] * pw_re[1:, :, None, :] - c_im[None] * pw_im[1:, :, None, :]
    cq_im = c_re[None] * pw_im[1:, :, None, :] + c_im[None] * pw_re[1:, :, None, :]
    to_cols = lambda m: m.transpose(1, 3, 0, 2).reshape(m.shape[1], S5_STATE, t * S5_GROUP)
    xm = jnp.concatenate([to_cols(cq_re), -to_cols(cq_im)], axis=1)
    ar, ai = pw_re[t], pw_im[t]
    a1 = jnp.concatenate([ar, ar, ar, ar], axis=-1)
    a2 = jnp.concatenate([-ai, ai, ai, -ai], axis=-1)
    per_slab = lambda m: m.reshape(-1, 1, S5_SLAB_GROUPS * 4 * S5_STATE)
    return mw.astype(BF16), toe, xm.astype(BF16), per_slab(a1), per_slab(a2)


def _s5_permutation():
    t, kk = S5_BLOCK, S5_GROUP
    row = jnp.arange(t * LANES)
    s, lane = row // LANES, row % LANES
    target = (lane // kk) * (t * kk) + s * kk + lane % kk
    return (target[:, None] == row[None, :]).astype(BF16)


def _s5_mixer(h, g_mix, lam_re, lam_im, log_dt, b_re, b_im, c_re, c_im):
    b, lp, d = h.shape
    t, kk = S5_BLOCK, S5_GROUP
    nblk = lp // t
    tm = ROW_TILE
    row_spec = pl.BlockSpec((None, tm, d), lambda i, j: (i, j, 0))
    a = pl.pallas_call(
        _norm_kernel,
        name='s5_norm',
        grid=(b, lp // tm),
        in_specs=[row_spec, _resident((1, d))],
        out_specs=row_spec,
        out_shape=jax.ShapeDtypeStruct((b, lp, d), F32),
        compiler_params=_cparams("parallel", "parallel"),
    )(h, g_mix[None])
    mw, toe, xm, a1, a2 = _s5_operators(lam_re, lam_im, log_dt, b_re, b_im, c_re, c_im)
    sel = _s5_permutation()
    gc = S5_SLAB_GROUPS
    width = 4 * S5_STATE
    nbs = S5_BATCH_ROWS if b % S5_BATCH_ROWS == 0 else 1
    slab_spec = pl.BlockSpec((nbs, lp, LANES), lambda i, j: (j, 0, i))
    grp_spec = lambda r, c: pl.BlockSpec((gc, r, c), lambda i, j: (i, 0, 0))
    coef_spec = pl.BlockSpec((None, 1, gc * width), lambda i, j: (i, 0, 0))
    return pl.pallas_call(
        _s5_kernel,
        name='s5_scan',
        grid=(d // LANES, b // nbs),
        in_specs=[slab_spec, _resident(sel.shape), grp_spec(t * kk, width), grp_spec(t * kk, t * kk),
                  grp_spec(2 * S5_STATE, t * kk), coef_spec, coef_spec],
        out_specs=slab_spec,
        out_shape=jax.ShapeDtypeStruct((b, lp, d), F32),
        scratch_shapes=[pltpu.VMEM((nbs * nblk, gc * width), F32), pltpu.VMEM((nbs * nblk, gc * width), F32)],
        compiler_params=_cparams("parallel", "parallel"),
    )(a, sel, mw, toe, xm, a1, a2)


def _ret_kernel(h_ref, gmix_ref, win_ref, cos_ref, sin_ref, gn_ref, o_ref, state_ref, q_s, k_s, v_s, g_s):
    t = pl.program_id(1)
    tm, d = h_ref.shape
    nh = RET_HEADS
    dk = d // nh
    dv = 2 * dk
    c = RET_CHUNK
    half = dk // 2

    @pl.when(t == 0)
    def _():
        state_ref[...] = jnp.zeros_like(state_ref)

    a = _rms(h_ref[...], gmix_ref[...]).astype(BF16)
    cos, sin = cos_ref[...], sin_ref[...]
    for hd in range(nh):
        for ref, col0, scl in ((q_s, 0, 1.0), (k_s, d, dk ** -0.5)):
            x = _dot(a, win_ref[:, col0 + hd * dk:col0 + (hd + 1) * dk])
            x1, x2 = x[:, :half], x[:, half:]
            ref[:, hd * dk:hd * dk + half] = (x1 * cos - x2 * sin) * scl
            ref[:, hd * dk + half:(hd + 1) * dk] = (x1 * sin + x2 * cos) * scl
        v_s[:, hd * dv:(hd + 1) * dv] = _dot(a, win_ref[:, 2 * d + hd * dv:2 * d + (hd + 1) * dv]).astype(BF16)
        g_s[:, hd * dv:(hd + 1) * dv] = _silu(_dot(a, win_ref[:, 4 * d + hd * dv:4 * d + (hd + 1) * dv]))
    ri = lax.broadcasted_iota(jnp.int32, (c, c), 0)
    ci_ = lax.broadcasted_iota(jnp.int32, (c, c), 1)
    diff = (ri - ci_).astype(F32)
    pos = lax.broadcasted_iota(jnp.int32, (c, 1), 0).astype(F32)

    def chunk_body(cc, _):
        rows = pl.ds(pl.multiple_of(cc * c, c), c)
        log_g = [math.log(1.0 - 2.0 ** (-5.0 - hd)) for hd in range(nh)]
        scores, inter = [], []
        for hd, lg in enumerate(log_g):
            q = q_s[rows, hd * dk:(hd + 1) * dk]
            k = k_s[rows, hd * dk:(hd + 1) * dk]
            v = v_s[rows, hd * dv:(hd + 1) * dv]
            st = state_ref[hd]
            decay = jnp.where(diff >= 0, jnp.exp(diff * lg), 0.0)
            scores.append((_dot_nt(q.astype(BF16), k.astype(BF16)) * decay).astype(BF16))
            q_in = (q * jnp.exp((pos + 1.0) * lg)).astype(BF16)
            k_out = (k * jnp.exp((c - 1.0 - pos) * lg)).astype(BF16)
            inter.append(_dot(q_in, st.astype(BF16)))
            state_ref[hd] = st * math.exp(c * lg) + _dot_tn(k_out, v)
        for hd in range(nh):
            o = _dot(scores[hd], v_s[rows, hd * dv:(hd + 1) * dv]) + inter[hd]
            mu = jnp.mean(o, axis=-1, keepdims=True)
            oc = o - mu
            var = jnp.mean(oc * oc, axis=-1, keepdims=True)
            o = oc * lax.rsqrt(var + EPS) * gn_ref[:, hd * dv:(hd + 1) * dv] * g_s[rows, hd * dv:(hd + 1) * dv]
            o_ref[rows, hd * dv:(hd + 1) * dv] = o.astype(o_ref.dtype)
        return 0

    lax.fori_loop(0, tm // c, chunk_body, 0, unroll=True)


def _ret_mixer(h, g_mix, w_in, gn_g):
    b, lp, d = h.shape
    nh = RET_HEADS
    dk = d // nh
    tm = ROW_TILE
    pos = jnp.maximum(jnp.arange(lp, dtype=F32) - PAD, 0.0)
    inv_freq = 1.0 / (ROPE_BASE ** (jnp.arange(0, dk, 2, dtype=F32) / dk))
    ang = pos[:, None] * inv_freq[None, :]
    cos, sin = jnp.cos(ang), jnp.sin(ang)
    row_spec = lambda width: pl.BlockSpec((None, tm, width), lambda i, t: (i, t, 0))
    tab_spec = pl.BlockSpec((tm, dk // 2), lambda i, t: (t, 0))
    win = w_in.astype(BF16)
    return pl.pallas_call(
        _ret_kernel,
        name='retention',
        grid=(b, lp // tm),
        in_specs=[row_spec(d), _resident((1, d)), _resident(win.shape), tab_spec, tab_spec,
                  _resident((1, 2 * d))],
        out_specs=row_spec(2 * d),
        out_shape=jax.ShapeDtypeStruct((b, lp, 2 * d), BF16),
        scratch_shapes=[pltpu.VMEM((nh, dk, 2 * dk), F32), pltpu.VMEM((tm, d), F32), pltpu.VMEM((tm, d), F32),
                        pltpu.VMEM((tm, 2 * d), BF16), pltpu.VMEM((tm, 2 * d), F32)],
        compiler_params=_cparams("arbitrary", "arbitrary"),
    )(h, g_mix[None], win, cos, sin, gn_g[None])


def kernel(x, meta_tokens, norm_mix_g, norm_ffn_g, mla_w_down, mla_cq_norm_g, mla_ckv_norm_g, mla_w_uq, mla_w_ukv, mla_q_head_g, mla_k_head_g, mla_w_o, hgrn_w_in, hgrn_lb_logits, hgrn_o_norm_g, hgrn_w_o, s5_lam_re, s5_lam_im, s5_log_dt, s5_b_re, s5_b_im, s5_c_re, s5_c_im, s5_d, s5_w_glu, ret_w_in, ret_gn_g, ret_w_o, ffn_w_up, ffn_conv_w, ffn_conv_b, ffn_w_down):
    b, seq, d = x.shape
    depth = norm_mix_g.shape[0]
    h = jnp.concatenate([jnp.zeros((b, PAD, d), x.dtype),
                         jnp.broadcast_to(meta_tokens[None].astype(x.dtype), (b, N_META, d)), x], axis=1)
    lb_cum = jnp.cumsum(jax.nn.softmax(hgrn_lb_logits.astype(F32), axis=0), axis=0)
    lb_all = lb_cum - lb_cum[0:1]
    for i in range(depth):
        m, j = i % 4, i // 4
        ffn_w = (norm_ffn_g[i][None], ffn_w_up[i].astype(BF16), ffn_conv_w[i], ffn_conv_b[i][None],
                 ffn_w_down[i].astype(BF16))
        g_mix = norm_mix_g[i]
        if m == 0:
            o = _mla_mixer(h, g_mix, mla_w_down[j], mla_cq_norm_g[j], mla_ckv_norm_g[j], mla_w_uq[j],
                           mla_w_ukv[j], mla_q_head_g[j], mla_k_head_g[j])
            h = _ffn_after_proj(h, o, mla_w_o[j].astype(BF16), ffn_w)
        elif m == 1:
            o = _hgrn_mixer(h, g_mix, hgrn_w_in[j], lb_all[i], hgrn_o_norm_g[j])
            h = _ffn_after_proj(h, o, hgrn_w_o[j].astype(BF16), ffn_w)
        elif m == 2:
            y = _s5_mixer(h, g_mix, s5_lam_re[j], s5_lam_im[j], s5_log_dt[j], s5_b_re[j], s5_b_im[j],
                          s5_c_re[j], s5_c_im[j])
            h = _ffn_after_s5(h, y, g_mix[None], s5_d[j][None], s5_w_glu[j].astype(BF16), ffn_w)
        else:
            o = _ret_mixer(h, g_mix, ret_w_in[j], ret_gn_g[j])
            h = _ffn_after_proj(h, o, ret_w_o[j].astype(BF16), ffn_w)
    return h[:, FRONT:]
```

```python
import math

import jax
import jax.numpy as jnp
from jax import lax
from jax.experimental import pallas as pl
from jax.experimental.pallas import tpu as pltpu

F32 = jnp.float32
BF16 = jnp.bfloat16

N_META = 16
EPS = 1e-6
NEG_INF = -1e30
ROPE_BASE = 10000.0
MLA_HEADS = 8
MLA_NOPE = 128
MLA_ROPE = 64
MLA_V = 128
MLA_QK = MLA_NOPE + MLA_ROPE
MLA_Q_LORA = 384
MLA_KV_LORA = 256
HGRN_HEADS = 8
HGRN_DK = 128
S5_GROUP = 16
S5_STATE = 64
RET_HEADS = 4

LANES = 128
PAD = LANES - N_META
FRONT = PAD + N_META
ROW_TILE = 384
FFN_ROW_TILE = 704
ATT_BLOCK = 256
ATT_SUBS = 4
ATT_MAX_SHIFT = 48.0
HGRN_CHUNK = 64
HGRN_REF = HGRN_CHUNK // 2 - 1
RET_CHUNK = 128
S5_BLOCK = 16
S5_SLAB_GROUPS = LANES // S5_GROUP
S5_BATCH_ROWS = 2
FFN_COLS = 256
VMEM_LIMIT = 56 * 1024 * 1024


def _cparams(*sem):
    return pltpu.CompilerParams(dimension_semantics=sem, vmem_limit_bytes=VMEM_LIMIT)


def _resident(shape):
    nd = len(shape)
    return pl.BlockSpec(shape, lambda *_: (0,) * nd, pipeline_mode=pl.Buffered(1))


def _rms(x, g):
    return x * lax.rsqrt(jnp.mean(x * x, axis=-1, keepdims=True) + EPS) * g


def _dot(a, b):
    return jnp.dot(a, b, preferred_element_type=F32)


def _dot_nt(a, b):
    return lax.dot_general(a, b, (((1,), (1,)), ((), ())), preferred_element_type=F32)


def _dot_tn(a, b):
    return lax.dot_general(a, b, (((0,), (0,)), ((), ())), preferred_element_type=F32)


def _sigmoid(x):
    return 1.0 / (1.0 + jnp.exp(-x))


def _silu(x):
    return x * _sigmoid(x)


def _gelu_tanh(x):
    return 0.5 * x * (1.0 + jnp.tanh(math.sqrt(2.0 / math.pi) * (x + 0.044715 * (x * x * x))))


def _tile_valid(t, tm):
    return t * tm + lax.broadcasted_iota(jnp.int32, (tm, 1), 0) >= PAD


def _ffn_tail(x, t, g_ref, wup_ref, cw_ref, cb_ref, wdn_ref, o_ref, carry_ref, act_ref):
    tm = x.shape[0]
    hidden = wdn_ref.shape[0]
    valid = _tile_valid(t, tm)
    x = jnp.where(valid, x, 0.0)
    a = _rms(x, g_ref[...]).astype(BF16)
    rid = lax.broadcasted_iota(jnp.int32, (tm, 1), 0)

    @pl.when(t == 0)
    def _():
        carry_ref[...] = jnp.zeros_like(carry_ref)

    def conv_slice(col0):
        cols = slice(col0, col0 + FFN_COLS)
        u = _dot(a, wup_ref[:, cols])
        prev = carry_ref[:, cols]
        p0, p1 = prev[6:7], prev[7:8]
        u1 = jnp.where(rid == 0, p1, pltpu.roll(u, 1, 0))
        u2 = jnp.where(rid == 0, p0, jnp.where(rid == 1, p1, pltpu.roll(u, 2, 0)))
        carry_ref[:, cols] = u[tm - 8:tm]
        w = cw_ref[:, cols]
        return w[0:1] * u2 + w[1:2] * u1 + w[2:3] * u + cb_ref[:, cols]

    for c in range(hidden // FFN_COLS):
        gate = conv_slice(c * FFN_COLS)
        val = conv_slice(hidden + c * FFN_COLS)
        act_ref[:, c * FFN_COLS:(c + 1) * FFN_COLS] = (_silu(gate) * val).astype(BF16)
    y = _dot(act_ref[...], wdn_ref[...])
    o_ref[...] = jnp.where(valid, x + y, 0.0)


def _ffn_proj_kernel(h_ref, o_in_ref, wo_ref, g_ref, wup_ref, cw_ref, cb_ref, wdn_ref,
                     o_ref, carry_ref, act_ref):
    t = pl.program_id(1)
    x = h_ref[...] + _dot(o_in_ref[...], wo_ref[...])
    _ffn_tail(x, t, g_ref, wup_ref, cw_ref, cb_ref, wdn_ref, o_ref, carry_ref, act_ref)


def _ffn_s5_kernel(h_ref, y_ref, gmix_ref, d_ref, wglu_ref, g_ref, wup_ref, cw_ref, cb_ref, wdn_ref,
                   o_ref, carry_ref, act_ref):
    t = pl.program_id(1)
    h = h_ref[...]
    d_model = h.shape[1]
    a = _rms(h, gmix_ref[...])
    z = _gelu_tanh(y_ref[...].astype(F32) + d_ref[...] * a).astype(BF16)
    val = _dot(z, wglu_ref[:, :d_model])
    gate = _dot(z, wglu_ref[:, d_model:])
    x = h + val * _sigmoid(gate)
    _ffn_tail(x, t, g_ref, wup_ref, cw_ref, cb_ref, wdn_ref, o_ref, carry_ref, act_ref)


def _ffn_call(kernel, h, lead_inputs, lead_specs, ffn_w):
    b, lp, d = h.shape
    g, wup, cw, cb, wdn = ffn_w
    hidden = wdn.shape[0]
    tm = FFN_ROW_TILE if lp % FFN_ROW_TILE == 0 else ROW_TILE
    row_spec = lambda width: pl.BlockSpec((None, tm, width), lambda i, t: (i, t, 0))
    return pl.pallas_call(
        kernel,
        name=kernel.__name__.strip('_'),
        grid=(b, lp // tm),
        in_specs=[row_spec(d)] + lead_specs(row_spec) + [
            _resident(g.shape), _resident(wup.shape), _resident(cw.shape), _resident(cb.shape),
            _resident(wdn.shape)],
        out_specs=row_spec(d),
        out_shape=jax.ShapeDtypeStruct((b, lp, d), F32),
        scratch_shapes=[pltpu.VMEM((8, 2 * hidden), F32), pltpu.VMEM((tm, hidden), BF16)],
        compiler_params=_cparams("arbitrary", "arbitrary"),
    )(h, *lead_inputs, g, wup, cw, cb, wdn)


def _ffn_after_proj(h, o_in, wo, ffn_w):
    return _ffn_call(_ffn_proj_kernel, h, (o_in, wo),
                     lambda row_spec: [row_spec(o_in.shape[-1]), _resident(wo.shape)], ffn_w)


def _ffn_after_s5(h, y, gmix, dskip, wglu, ffn_w):
    return _ffn_call(_ffn_s5_kernel, h, (y, gmix, dskip, wglu),
                     lambda row_spec: [row_spec(y.shape[-1]), _resident(gmix.shape), _resident(dskip.shape),
                                       _resident(wglu.shape)], ffn_w)


def _rope_swap(r):
    lane = lax.broadcasted_iota(jnp.int32, r.shape, 1)
    half = MLA_ROPE // 2
    return jnp.where(lane < half, pltpu.roll(r, LANES - half, 1), pltpu.roll(r, half, 1))


def _mla_qkv_kernel(h_ref, gmix_ref, wd_ref, gcq_ref, gckv_ref, wuq_ref, wukv_ref,
                    gqn_ref, gqr_ref, gkn_ref, gkr_ref, qone_ref, kshift_ref, cos_ref, sin_ref,
                    q_ref, k_ref, v_ref):
    a = _rms(h_ref[...], gmix_ref[...]).astype(BF16)
    down = _dot(a, wd_ref[...])
    kv_lo = MLA_Q_LORA + MLA_KV_LORA
    cq = _rms(down[:, :MLA_Q_LORA], gcq_ref[...]).astype(BF16)
    ckv = _rms(down[:, MLA_Q_LORA:kv_lo], gckv_ref[...]).astype(BF16)
    kpe = down[:, kv_lo:kv_lo + LANES]
    q = _dot(cq, wuq_ref[...])
    kv = _dot(ckv, wukv_ref[...])
    cos, sin = cos_ref[...], sin_ref[...]

    def rope(r):
        return r * cos + _rope_swap(r) * sin

    hw = 2 * LANES
    pair_ones = (lax.broadcasted_iota(jnp.int32, (hw, hw), 0) // LANES
                 == lax.broadcasted_iota(jnp.int32, (hw, hw), 1) // LANES).astype(BF16)

    def row_sums(x0, x1):
        both = _dot(jnp.concatenate([x0, x1], axis=1).astype(BF16), pair_ones)
        return both[:, :LANES], both[:, LANES:]

    sq_pe = kpe * kpe
    ss_pe, _ = row_sums(sq_pe, sq_pe)
    k_rope = rope(kpe * gkr_ref[...])
    scale = MLA_QK ** -0.5 * math.log2(math.e)
    for h0 in range(0, MLA_HEADS, 2):
        pair = (h0, h0 + 1)
        qn = [q[:, hd * hw:hd * hw + LANES] for hd in pair]
        qr = [q[:, hd * hw + LANES:(hd + 1) * hw] for hd in pair]
        kn = [kv[:, hd * hw:hd * hw + LANES] for hd in pair]
        ss_q = row_sums(qn[0] * qn[0] + qr[0] * qr[0], qn[1] * qn[1] + qr[1] * qr[1])
        ss_k = row_sums(kn[0] * kn[0], kn[1] * kn[1])
        for n, hd in enumerate(pair):
            inv_q = lax.rsqrt(ss_q[n] / MLA_QK + EPS) * scale
            q_ref[:, hd * hw:hd * hw + LANES] = (qn[n] * inv_q * gqn_ref[...]).astype(BF16)
            q_ref[:, hd * hw + LANES:(hd + 1) * hw] = (
                rope(qr[n] * gqr_ref[...]) * inv_q + qone_ref[...]).astype(BF16)
            inv_k = lax.rsqrt((ss_k[n] + ss_pe) / MLA_QK + EPS)
            k_ref[:, hd * hw:hd * hw + LANES] = (kn[n] * inv_k * gkn_ref[...]).astype(BF16)
            k_ref[:, hd * hw + LANES:(hd + 1) * hw] = (k_rope * inv_k + kshift_ref[...]).astype(BF16)
            v_ref[:, hd * LANES:(hd + 1) * LANES] = kv[:, hd * hw + LANES:(hd + 1) * hw].astype(BF16)


def _attn_kernel(q_ref, k_ref, v_ref, o_ref, ve_s):
    lp, dv = v_ref.shape
    blk, subs = ATT_BLOCK, ATT_SUBS
    n_sup = (lp - FRONT) // (blk * subs)
    ve_s[:, 0:dv] = v_ref[...]
    ve_s[:, dv:2 * dv] = jnp.ones((lp, dv), BF16)
    k0, v0 = k_ref[0:FRONT], ve_s[0:FRONT]
    key_ok0 = lax.broadcasted_iota(jnp.int32, (1, FRONT), 1) >= PAD
    chunk = FRONT // 2
    diag_ok = (lax.broadcasted_iota(jnp.int32, (blk, blk), 0) // chunk
               >= lax.broadcasted_iota(jnp.int32, (blk, blk), 1) // chunk)

    def first_block(q):
        s = jnp.where(key_ok0, _dot_nt(q, k0), NEG_INF)
        m = jnp.max(s, axis=-1, keepdims=True)
        return m, _dot(jnp.exp2(s - m).astype(BF16), v0)

    def update(carry, s, vb):
        m, acc = carry
        m_new = jnp.maximum(m, jnp.max(s, axis=-1, keepdims=True))
        p = jnp.exp2(s - m_new).astype(BF16)
        return m_new, jnp.exp2(m - m_new) * acc + _dot(p, vb)

    def finish(carry):
        acc = carry[1]
        return (acc[:, 0:dv] / acc[:, dv:2 * dv]).astype(o_ref.dtype)

    o_ref[0:FRONT] = finish(first_block(q_ref[0:FRONT]))

    def sup_body(i, _):
        base = FRONT + i * (blk * subs)
        rows = [pl.ds(pl.multiple_of(base + a * blk, LANES), blk) for a in range(subs)]

        def key_rows(j):
            return pl.ds(pl.multiple_of(FRONT + j * blk, LANES), blk)

        def scores(j):
            kb = k_ref[key_rows(j)]
            return tuple(_dot_nt(q_ref[rows[a]], kb) for a in range(subs))

        def kv_body(j, state):
            carries, s_cur = state
            s_next = scores(j + 1)
            vb = ve_s[key_rows(j)]
            return tuple(update(carries[a], s_cur[a], vb) for a in range(subs)), s_next

        carries, s_cur = lax.fori_loop(0, i * subs, kv_body,
                                       (tuple(first_block(q_ref[r]) for r in rows), scores(0)))
        carries = list(carries)
        for c in range(subs):
            if c > 0:
                kb = k_ref[rows[c]]
                s_cur = [None] * c + [_dot_nt(q_ref[rows[a]], kb) for a in range(c, subs)]
            vb = ve_s[rows[c]]
            for a in range(c, subs):
                s = jnp.where(diag_ok, s_cur[a], NEG_INF) if a == c else s_cur[a]
                carries[a] = update(carries[a], s, vb)
        for a in range(subs):
            o_ref[rows[a]] = finish(carries[a])
        return 0

    lax.fori_loop(0, n_sup, sup_body, 0)


def _attn_bounded_kernel(q_ref, k_ref, v_ref, o_ref, ve_s):
    lp, dv = v_ref.shape
    blk, subs = ATT_BLOCK, ATT_SUBS
    n_sup = (lp - FRONT) // (blk * subs)
    ve_s[:, 0:dv] = v_ref[...]
    ve_s[:, dv:2 * dv] = jnp.ones((lp, dv), BF16)
    k0, v0 = k_ref[0:FRONT], ve_s[0:FRONT]
    key_ok0 = lax.broadcasted_iota(jnp.int32, (1, FRONT), 1) >= PAD
    chunk = FRONT // 2
    per_blk = blk // chunk
    lag = (lax.broadcasted_iota(jnp.int32, (blk, 2 * blk), 1) // chunk
           - lax.broadcasted_iota(jnp.int32, (blk, 2 * blk), 0) // chunk)

    def front_part(q):
        return _dot(jnp.exp2(jnp.where(key_ok0, _dot_nt(q, k0), NEG_INF)).astype(BF16), v0)

    def finish(acc):
        return (acc[:, 0:dv] / acc[:, dv:2 * dv]).astype(o_ref.dtype)

    o_ref[0:FRONT] = finish(front_part(q_ref[0:FRONT]))

    def sup_body(i, _):
        base = FRONT + i * (blk * subs)
        rows = [pl.ds(pl.multiple_of(base + a * blk, LANES), blk) for a in range(subs)]

        def key_rows(j):
            return pl.ds(pl.multiple_of(FRONT + j * 2 * blk, LANES), 2 * blk)

        def weights(j, first=0):
            kb = k_ref[key_rows(j)]
            out = []
            for a in range(first, subs):
                limit = (i * subs + a - 2 * j) * per_blk
                out.append(jnp.exp2(jnp.where(lag <= limit, _dot_nt(q_ref[rows[a]], kb), NEG_INF)).astype(BF16))
            return tuple(out)

        def kv_body(j, state):
            accs, p_cur = state
            p_next = weights(j + 1)
            vb = ve_s[key_rows(j)]
            return tuple(acc + _dot(p, vb) for acc, p in zip(accs, p_cur)), p_next

        own = i * (subs // 2)
        accs, p_cur = lax.fori_loop(0, own, kv_body,
                                    (tuple(front_part(q_ref[r]) for r in rows), weights(0)))
        p_last = weights(own + 1, first=subs // 2)
        vb, vb_last = ve_s[key_rows(own)], ve_s[key_rows(own + 1)]
        for a in range(subs):
            acc = accs[a] + _dot(p_cur[a], vb)
            if a >= subs // 2:
                acc = acc + _dot(p_last[a - subs // 2], vb_last)
            o_ref[rows[a]] = finish(acc)
        return 0

    lax.fori_loop(0, n_sup, sup_body, 0)


def _mla_rope_tables(lp):
    half = MLA_ROPE // 2
    pos = jnp.maximum(jnp.arange(lp, dtype=F32) - PAD, 0.0)
    inv_freq = 1.0 / (ROPE_BASE ** (jnp.arange(0, MLA_ROPE, 2, dtype=F32) / MLA_ROPE))
    ang = pos[:, None] * inv_freq[None, :]
    c, s = jnp.cos(ang), jnp.sin(ang)
    z = jnp.zeros((lp, LANES - MLA_ROPE), F32)
    return jnp.concatenate([c, c, z], axis=1), jnp.concatenate([-s, s, z], axis=1)


def _mla_mixer(h, g_mix, w_down, g_cq, g_ckv, w_uq, w_ukv, g_qhead, g_khead):
    b, lp, d = h.shape
    nh, hw = MLA_HEADS, 2 * LANES
    kv_lo = MLA_Q_LORA + MLA_KV_LORA
    wd = jnp.pad(w_down, ((0, 0), (0, kv_lo + LANES - w_down.shape[1]))).astype(BF16)
    wuq = jnp.pad(w_uq.reshape(MLA_Q_LORA, nh, MLA_QK), ((0, 0), (0, 0), (0, hw - MLA_QK)))
    wuq = wuq.reshape(MLA_Q_LORA, nh * hw).astype(BF16)
    wukv = w_ukv.astype(BF16)
    zpad = jnp.zeros((LANES - MLA_ROPE,), F32)
    gqn, gkn = g_qhead[None, :MLA_NOPE], g_khead[None, :MLA_NOPE]
    gqr = jnp.concatenate([g_qhead[MLA_NOPE:], zpad])[None]
    gkr = jnp.concatenate([g_khead[MLA_NOPE:], zpad])[None]
    cos, sin = _mla_rope_tables(lp)
    shift = (math.log2(math.e) * math.sqrt(MLA_QK)) * jnp.max(jnp.abs(g_qhead)) * jnp.max(jnp.abs(g_khead))
    spare = jnp.arange(LANES) == MLA_ROPE
    qone = jnp.where(spare, 1.0, 0.0).astype(F32)[None]
    kshift = jnp.where(spare, -shift, 0.0).astype(F32)[None]
    tm = ROW_TILE
    row_spec = lambda width: pl.BlockSpec((None, tm, width), lambda i, t: (i, t, 0))
    tab_spec = pl.BlockSpec((tm, LANES), lambda i, t: (t, 0))
    small = [g_mix[None], wd, g_cq[None], g_ckv[None], wuq, wukv, gqn, gqr, gkn, gkr, qone, kshift]
    q, k, v = pl.pallas_call(
        _mla_qkv_kernel,
        name='mla_qkv',
        grid=(b, lp // tm),
        in_specs=[row_spec(d)] + [_resident(w.shape) for w in small] + [tab_spec, tab_spec],
        out_specs=[row_spec(nh * hw), row_spec(nh * hw), row_spec(nh * MLA_V)],
        out_shape=[jax.ShapeDtypeStruct((b, lp, nh * hw), BF16), jax.ShapeDtypeStruct((b, lp, nh * hw), BF16),
                   jax.ShapeDtypeStruct((b, lp, nh * MLA_V), BF16)],
        compiler_params=_cparams("parallel", "parallel"),
    )(h, *small, cos, sin)
    head_spec = lambda width: pl.BlockSpec((None, lp, width), lambda i, j: (i, 0, j))

    def attend(body, name):
        return pl.pallas_call(
            body,
            name=name,
            grid=(b, nh),
            in_specs=[head_spec(hw), head_spec(hw), head_spec(MLA_V)],
            out_specs=head_spec(MLA_V),
            out_shape=jax.ShapeDtypeStruct((b, lp, nh * MLA_V), BF16),
            scratch_shapes=[pltpu.VMEM((lp, 2 * MLA_V), BF16)],
            compiler_params=_cparams("parallel", "parallel"),
        )

    return lax.cond(shift <= ATT_MAX_SHIFT,
                    attend(_attn_bounded_kernel, 'mla_attention_bounded'),
                    attend(_attn_kernel, 'mla_attention'), q, k, v)


def _hgrn_kernel(h_ref, gmix_ref, win_ref, lb_ref, go_ref, tri_ref, o_ref,
                 state_ref, q_s, k_s, v_s, lf_s, gate_s):
    t = pl.program_id(1)
    tm, d = h_ref.shape
    dk = HGRN_DK
    c = HGRN_CHUNK

    @pl.when(t == 0)
    def _():
        state_ref[...] = jnp.zeros_like(state_ref)

    a = _rms(h_ref[...], gmix_ref[...]).astype(BF16)
    lb = lb_ref[...]
    q_s[...] = _silu(_dot(a, win_ref[:, 0:d]))
    forget = lb + (1.0 - lb) * _sigmoid(_dot(a, win_ref[:, d:2 * d]))
    lf_s[...] = jnp.log(forget)
    k_s[...] = 1.0 - forget
    v_s[...] = _dot(a, win_ref[:, 2 * d:3 * d]).astype(BF16)
    gate_s[...] = _silu(_dot(a, win_ref[:, 3 * d:4 * d]))
    causal = (lax.broadcasted_iota(jnp.int32, (c, c), 0) >= lax.broadcasted_iota(jnp.int32, (c, c), 1))
    tri = tri_ref[...]

    def chunk_body(ci, _):
        rows = pl.ds(pl.multiple_of(ci * c, c), c)
        lf = lf_s[rows]
        lf_hi = lf.astype(BF16)
        lf_lo = (lf - lf_hi.astype(F32)).astype(BF16)
        cum = _dot(tri, lf_hi) + _dot(tri, lf_lo)
        ref_row = cum[HGRN_REF:HGRN_REF + 1]
        last = cum[c - 1:c]
        q, k, v = q_s[rows], k_s[rows], v_s[rows]
        q_rel = (q * jnp.exp(cum - ref_row)).astype(BF16)
        k_rel = (k * jnp.exp(ref_row - cum)).astype(BF16)
        q_dec = (q * jnp.exp(cum)).astype(BF16)
        k_tail = (k * jnp.exp(last - cum)).astype(BF16)
        decay = jnp.exp(last)
        gate = gate_s[rows]
        heads = [slice(hd * dk, (hd + 1) * dk) for hd in range(d // dk)]
        attn = [jnp.where(causal, _dot_nt(q_rel[:, sl], k_rel[:, sl]), 0.0).astype(BF16) for sl in heads]
        inter = [_dot_nt(q_dec[:, sl], state_ref[hd].astype(BF16)) for hd, sl in enumerate(heads)]
        for hd, sl in enumerate(heads):
            state_ref[hd] = state_ref[hd] * decay[:, sl] + _dot_tn(v[:, sl], k_tail[:, sl])
        for hd, sl in enumerate(heads):
            o = _dot(attn[hd], v[:, sl]) + inter[hd]
            o_ref[rows, sl] = (_rms(o, go_ref[...]) * gate[:, sl]).astype(o_ref.dtype)
        return 0

    lax.fori_loop(0, tm // c, chunk_body, 0, unroll=3)


def _hgrn_mixer(h, g_mix, w_in, lb, g_o):
    b, lp, d = h.shape
    tm = ROW_TILE
    c = HGRN_CHUNK
    tri = (jnp.arange(c)[:, None] >= jnp.arange(c)[None, :]).astype(BF16)
    row_spec = pl.BlockSpec((None, tm, d), lambda i, t: (i, t, 0))
    small = [g_mix[None], w_in.astype(BF16), lb[None], g_o[None], tri]
    return pl.pallas_call(
        _hgrn_kernel,
        name='hgrn2',
        grid=(b, lp // tm),
        in_specs=[row_spec] + [_resident(w.shape) for w in small],
        out_specs=row_spec,
        out_shape=jax.ShapeDtypeStruct((b, lp, d), BF16),
        scratch_shapes=[pltpu.VMEM((d // HGRN_DK, d // HGRN_HEADS, HGRN_DK), F32),
                        pltpu.VMEM((tm, d), F32), pltpu.VMEM((tm, d), F32), pltpu.VMEM((tm, d), BF16),
                        pltpu.VMEM((tm, d), F32), pltpu.VMEM((tm, d), F32)],
        compiler_params=_cparams("arbitrary", "arbitrary"),
    )(h, *small)


def _norm_kernel(h_ref, g_ref, o_ref):
    o_ref[...] = _rms(h_ref[...], g_ref[...]).astype(o_ref.dtype)


def _s5_kernel(a_ref, perm_ref, mw_ref, toe_ref, xm_ref, a1_ref, a2_ref, y_ref, w_s, yt_s):
    nbs, lp, _ = a_ref.shape
    t = S5_BLOCK
    nblk = lp // t
    gc = xm_ref.shape[0]
    width = 4 * S5_STATE
    half = width // 2
    steps = jnp.concatenate(
        [jnp.concatenate([a_ref[bb, pl.ds(s, nblk, stride=t), :] for s in range(t)], axis=1) for bb in range(nbs)],
        axis=0).astype(BF16)
    u_all = _dot(steps, perm_ref[...]).astype(BF16)
    for g in range(gc):
        u = u_all[:, g * width:(g + 1) * width]
        w_s[:, g * width:(g + 1) * width] = _dot(u, mw_ref[g])
        yt_s[:, g * width:(g + 1) * width] = _dot(u, toe_ref[g])
    a1, a2 = a1_ref[...], a2_ref[...]

    def swap_halves(z):
        parts = []
        for g in range(gc):
            parts += [z[:, g * width + half:(g + 1) * width], z[:, g * width:g * width + half]]
        return jnp.concatenate(parts, axis=1)

    def step(blk, zs):
        nxt = []
        for bb, z in enumerate(zs):
            row = pl.ds(bb * nblk + blk, 1)
            w = w_s[row, :]
            w_s[row, :] = z
            nxt.append(a1 * z + a2 * swap_halves(z) + w)
        return tuple(nxt)

    lax.fori_loop(0, nblk, step, tuple(jnp.zeros((1, gc * width), F32) for _ in range(nbs)))
    ys = []
    for g in range(gc):
        x_start = w_s[:, g * width:g * width + half].astype(BF16)
        ys.append((yt_s[:, g * width:(g + 1) * width] + _dot(x_start, xm_ref[g])).astype(BF16))
    out = _dot_nt(jnp.concatenate(ys, axis=1), perm_ref[...])
    for bb in range(nbs):
        for s in range(t):
            y_ref[bb, pl.ds(s, nblk, stride=t), :] = out[bb * nblk:(bb + 1) * nblk, s * LANES:(s + 1) * LANES]


def _s5_operators(lam_re, lam_im, log_dt, b_re, b_im, c_re, c_im):
    hp = lax.Precision.HIGHEST
    t = S5_BLOCK
    dt = jnp.exp(log_dt)[:, None]
    mag = jnp.exp(lam_re * dt)
    abar_re, abar_im = mag * jnp.cos(lam_im * dt), mag * jnp.sin(lam_im * dt)
    den = lam_re * lam_re + lam_im * lam_im
    zoh_re = ((abar_re - 1.0) * lam_re + abar_im * lam_im) / den
    zoh_im = (abar_im * lam_re - (abar_re - 1.0) * lam_im) / den
    bb_re = zoh_re[..., None] * b_re - zoh_im[..., None] * b_im
    bb_im = zoh_re[..., None] * b_im + zoh_im[..., None] * b_re
    steps = jnp.arange(t + 1, dtype=F32)[:, None, None] * dt[None]
    pmag = jnp.exp(lam_re[None] * steps)
    pw_re, pw_im = pmag * jnp.cos(lam_im[None] * steps), pmag * jnp.sin(lam_im[None] * steps)
    cp_re = c_re[None] * pw_re[:t, :, None, :] - c_im[None] * pw_im[:t, :, None, :]
    cp_im = c_re[None] * pw_im[:t, :, None, :] + c_im[None] * pw_re[:t, :, None, :]
    taps = jnp.einsum("tgjp,gpk->gktj", jnp.concatenate([cp_re, -cp_im], axis=-1),
                      jnp.concatenate([bb_re, bb_im], axis=1), precision=hp)
    ar_t = jnp.arange(t)
    place = (ar_t[None, :, None] - ar_t[:, None, None] == ar_t[None, None, :]).astype(BF16)
    toe = jnp.einsum("stu,gkuj->gsktj", place, taps.astype(BF16)).reshape(-1, t * S5_GROUP, t * S5_GROUP)
    rev_re, rev_im = pw_re[t - 1::-1][:t], pw_im[t - 1::-1][:t]
    w_re = rev_re[..., None] * bb_re[None] - rev_im[..., None] * bb_im[None]
    w_im = rev_re[..., None] * bb_im[None] + rev_im[..., None] * bb_re[None]
    to_rows = lambda m: m.transpose(1, 0, 3, 2).reshape(m.shape[1], t * S5_GROUP, S5_STATE)
    w_re, w_im = to_rows(w_re), to_rows(w_im)
    mw = jnp.concatenate([w_re, w_im, w_im, w_re], axis=-1)
    cq_re = c_re[None] * pw_re[1:, :, None, :] - c_im[None] * pw_im[1:, :, None, :]
    cq_im = c_re[None] * pw_im[1:, :, None, :] + c_im[None] * pw_re[1:, :, None, :]
    to_cols = lambda m: m.transpose(1, 3, 0, 2).reshape(m.shape[1], S5_STATE, t * S5_GROUP)
    xm = jnp.concatenate([to_cols(cq_re), -to_cols(cq_im)], axis=1)
    ar, ai = pw_re[t], pw_im[t]
    a1 = jnp.concatenate([ar, ar, ar, ar], axis=-1)
    a2 = jnp.concatenate([-ai, ai, ai, -ai], axis=-1)
    per_slab = lambda m: m.reshape(-1, 1, S5_SLAB_GROUPS * 4 * S5_STATE)
    return mw.astype(BF16), toe, xm.astype(BF16), per_slab(a1), per_slab(a2)


def _s5_permutation():
    t, kk = S5_BLOCK, S5_GROUP
    row = jnp.arange(t * LANES)
    s, lane = row // LANES, row % LANES
    target = (lane // kk) * (t * kk) + s * kk + lane % kk
    return (target[:, None] == row[None, :]).astype(BF16)


def _s5_mixer(h, g_mix, lam_re, lam_im, log_dt, b_re, b_im, c_re, c_im):
    b, lp, d = h.shape
    t, kk = S5_BLOCK, S5_GROUP
    nblk = lp // t
    tm = ROW_TILE
    row_spec = pl.BlockSpec((None, tm, d), lambda i, j: (i, j, 0))
    a = pl.pallas_call(
        _norm_kernel,
        name='s5_norm',
        grid=(b, lp // tm),
        in_specs=[row_spec, _resident((1, d))],
        out_specs=row_spec,
        out_shape=jax.ShapeDtypeStruct((b, lp, d), F32),
        compiler_params=_cparams("parallel", "parallel"),
    )(h, g_mix[None])
    mw, toe, xm, a1, a2 = _s5_operators(lam_re, lam_im, log_dt, b_re, b_im, c_re, c_im)
    sel = _s5_permutation()
    gc = S5_SLAB_GROUPS
    width = 4 * S5_STATE
    nbs = S5_BATCH_ROWS if b % S5_BATCH_ROWS == 0 else 1
    slab_spec = pl.BlockSpec((nbs, lp, LANES), lambda i, j: (j, 0, i))
    grp_spec = lambda r, c: pl.BlockSpec((gc, r, c), lambda i, j: (i, 0, 0))
    coef_spec = pl.BlockSpec((None, 1, gc * width), lambda i, j: (i, 0, 0))
    return pl.pallas_call(
        _s5_kernel,
        name='s5_scan',
        grid=(d // LANES, b // nbs),
        in_specs=[slab_spec, _resident(sel.shape), grp_spec(t * kk, width), grp_spec(t * kk, t * kk),
                  grp_spec(2 * S5_STATE, t * kk), coef_spec, coef_spec],
        out_specs=slab_spec,
        out_shape=jax.ShapeDtypeStruct((b, lp, d), F32),
        scratch_shapes=[pltpu.VMEM((nbs * nblk, gc * width), F32), pltpu.VMEM((nbs * nblk, gc * width), F32)],
        compiler_params=_cparams("parallel", "parallel"),
    )(a, sel, mw, toe, xm, a1, a2)


def _ret_kernel(h_ref, gmix_ref, win_ref, cos_ref, sin_ref, gn_ref, o_ref, state_ref, q_s, k_s, v_s, g_s):
    t = pl.program_id(1)
    tm, d = h_ref.shape
    nh = RET_HEADS
    dk = d // nh
    dv = 2 * dk
    c = RET_CHUNK
    half = dk // 2

    @pl.when(t == 0)
    def _():
        state_ref[...] = jnp.zeros_like(state_ref)

    a = _rms(h_ref[...], gmix_ref[...]).astype(BF16)
    cos, sin = cos_ref[...], sin_ref[...]
    for hd in range(nh):
        for ref, col0, scl in ((q_s, 0, 1.0), (k_s, d, dk ** -0.5)):
            x = _dot(a, win_ref[:, col0 + hd * dk:col0 + (hd + 1) * dk])
            x1, x2 = x[:, :half], x[:, half:]
            ref[:, hd * dk:hd * dk + half] = (x1 * cos - x2 * sin) * scl
            ref[:, hd * dk + half:(hd + 1) * dk] = (x1 * sin + x2 * cos) * scl
        v_s[:, hd * dv:(hd + 1) * dv] = _dot(a, win_ref[:, 2 * d + hd * dv:2 * d + (hd + 1) * dv]).astype(BF16)
        g_s[:, hd * dv:(hd + 1) * dv] = _silu(_dot(a, win_ref[:, 4 * d + hd * dv:4 * d + (hd + 1) * dv]))
    ri = lax.broadcasted_iota(jnp.int32, (c, c), 0)
    ci_ = lax.broadcasted_iota(jnp.int32, (c, c), 1)
    diff = (ri - ci_).astype(F32)
    pos = lax.broadcasted_iota(jnp.int32, (c, 1), 0).astype(F32)

    def chunk_body(cc, _):
        rows = pl.ds(pl.multiple_of(cc * c, c), c)
        log_g = [math.log(1.0 - 2.0 ** (-5.0 - hd)) for hd in range(nh)]
        scores, inter = [], []
        for hd, lg in enumerate(log_g):
            q = q_s[rows, hd * dk:(hd + 1) * dk]
            k = k_s[rows, hd * dk:(hd + 1) * dk]
            v = v_s[rows, hd * dv:(hd + 1) * dv]
            st = state_ref[hd]
            decay = jnp.where(diff >= 0, jnp.exp(diff * lg), 0.0)
            scores.append((_dot_nt(q.astype(BF16), k.astype(BF16)) * decay).astype(BF16))
            q_in = (q * jnp.exp((pos + 1.0) * lg)).astype(BF16)
            k_out = (k * jnp.exp((c - 1.0 - pos) * lg)).astype(BF16)
            inter.append(_dot(q_in, st.astype(BF16)))
            state_ref[hd] = st * math.exp(c * lg) + _dot_tn(k_out, v)
        for hd in range(nh):
            o = _dot(scores[hd], v_s[rows, hd * dv:(hd + 1) * dv]) + inter[hd]
            mu = jnp.mean(o, axis=-1, keepdims=True)
            oc = o - mu
            var = jnp.mean(oc * oc, axis=-1, keepdims=True)
            o = oc * lax.rsqrt(var + EPS) * gn_ref[:, hd * dv:(hd + 1) * dv] * g_s[rows, hd * dv:(hd + 1) * dv]
            o_ref[rows, hd * dv:(hd + 1) * dv] = o.astype(o_ref.dtype)
        return 0

    lax.fori_loop(0, tm // c, chunk_body, 0, unroll=True)


def _ret_mixer(h, g_mix, w_in, gn_g):
    b, lp, d = h.shape
    nh = RET_HEADS
    dk = d // nh
    tm = ROW_TILE
    pos = jnp.maximum(jnp.arange(lp, dtype=F32) - PAD, 0.0)
    inv_freq = 1.0 / (ROPE_BASE ** (jnp.arange(0, dk, 2, dtype=F32) / dk))
    ang = pos[:, None] * inv_freq[None, :]
    cos, sin = jnp.cos(ang), jnp.sin(ang)
    row_spec = lambda width: pl.BlockSpec((None, tm, width), lambda i, t: (i, t, 0))
    tab_spec = pl.BlockSpec((tm, dk // 2), lambda i, t: (t, 0))
    win = w_in.astype(BF16)
    return pl.pallas_call(
        _ret_kernel,
        name='retention',
        grid=(b, lp // tm),
        in_specs=[row_spec(d), _resident((1, d)), _resident(win.shape), tab_spec, tab_spec,
                  _resident((1, 2 * d))],
        out_specs=row_spec(2 * d),
        out_shape=jax.ShapeDtypeStruct((b, lp, 2 * d), BF16),
        scratch_shapes=[pltpu.VMEM((nh, dk, 2 * dk), F32), pltpu.VMEM((tm, d), F32), pltpu.VMEM((tm, d), F32),
                        pltpu.VMEM((tm, 2 * d), BF16), pltpu.VMEM((tm, 2 * d), F32)],
        compiler_params=_cparams("arbitrary", "arbitrary"),
    )(h, g_mix[None], win, cos, sin, gn_g[None])


def kernel(x, meta_tokens, norm_mix_g, norm_ffn_g, mla_w_down, mla_cq_norm_g, mla_ckv_norm_g, mla_w_uq, mla_w_ukv, mla_q_head_g, mla_k_head_g, mla_w_o, hgrn_w_in, hgrn_lb_logits, hgrn_o_norm_g, hgrn_w_o, s5_lam_re, s5_lam_im, s5_log_dt, s5_b_re, s5_b_im, s5_c_re, s5_c_im, s5_d, s5_w_glu, ret_w_in, ret_gn_g, ret_w_o, ffn_w_up, ffn_conv_w, ffn_conv_b, ffn_w_down):
    b, seq, d = x.shape
    depth = norm_mix_g.shape[0]
    h = jnp.concatenate([jnp.zeros((b, PAD, d), x.dtype),
                         jnp.broadcast_to(meta_tokens[None].astype(x.dtype), (b, N_META, d)), x], axis=1)
    lb_cum = jnp.cumsum(jax.nn.softmax(hgrn_lb_logits.astype(F32), axis=0), axis=0)
    lb_all = lb_cum - lb_cum[0:1]
    for i in range(depth):
        m, j = i % 4, i // 4
        ffn_w = (norm_ffn_g[i][None], ffn_w_up[i].astype(BF16), ffn_conv_w[i], ffn_conv_b[i][None],
                 ffn_w_down[i].astype(BF16))
        g_mix = norm_mix_g[i]
        if m == 0:
            o = _mla_mixer(h, g_mix, mla_w_down[j], mla_cq_norm_g[j], mla_ckv_norm_g[j], mla_w_uq[j],
                           mla_w_ukv[j], mla_q_head_g[j], mla_k_head_g[j])
            h = _ffn_after_proj(h, o, mla_w_o[j].astype(BF16), ffn_w)
        elif m == 1:
            o = _hgrn_mixer(h, g_mix, hgrn_w_in[j], lb_all[i], hgrn_o_norm_g[j])
            h = _ffn_after_proj(h, o, hgrn_w_o[j].astype(BF16), ffn_w)
        elif m == 2:
            y = _s5_mixer(h, g_mix, s5_lam_re[j], s5_lam_im[j], s5_log_dt[j], s5_b_re[j], s5_b_im[j],
                          s5_c_re[j], s5_c_im[j])
            h = _ffn_after_s5(h, y, g_mix[None], s5_d[j][None], s5_w_glu[j].astype(BF16), ffn_w)
        else:
            o = _ret_mixer(h, g_mix, ret_w_in[j], ret_gn_g[j])
            h = _ffn_after_proj(h, o, ret_w_o[j].astype(BF16), ffn_w)
    return h[:, FRONT:]
```

```python
import math

import jax
import jax.numpy as jnp
from jax import lax
from jax.experimental import pallas as pl
from jax.experimental.pallas import tpu as pltpu

F32 = jnp.float32
BF16 = jnp.bfloat16

N_META = 16
EPS = 1e-6
NEG_INF = -1e30
ROPE_BASE = 10000.0
MLA_HEADS = 8
MLA_NOPE = 128
MLA_ROPE = 64
MLA_V = 128
MLA_QK = MLA_NOPE + MLA_ROPE
MLA_Q_LORA = 384
MLA_KV_LORA = 256
HGRN_HEADS = 8
HGRN_DK = 128
S5_GROUP = 16
S5_STATE = 64
RET_HEADS = 4

LANES = 128
PAD = LANES - N_META
FRONT = PAD + N_META
ROW_TILE = 384
FFN_ROW_TILE = 704
ATT_BLOCK = 256
ATT_SUBS = 4
ATT_MAX_SHIFT = 48.0
HGRN_CHUNK = 64
HGRN_REF = HGRN_CHUNK // 2 - 1
RET_CHUNK = 128
S5_BLOCK = 16
S5_SLAB_GROUPS = LANES // S5_GROUP
S5_BATCH_ROWS = 2
FFN_COLS = 256
VMEM_LIMIT = 56 * 1024 * 1024


def _cparams(*sem):
    return pltpu.CompilerParams(dimension_semantics=sem, vmem_limit_bytes=VMEM_LIMIT)


def _resident(shape):
    nd = len(shape)
    return pl.BlockSpec(shape, lambda *_: (0,) * nd, pipeline_mode=pl.Buffered(1))


def _rms(x, g):
    return x * lax.rsqrt(jnp.mean(x * x, axis=-1, keepdims=True) + EPS) * g


def _dot(a, b):
    return jnp.dot(a, b, preferred_element_type=F32)


def _dot_nt(a, b):
    return lax.dot_general(a, b, (((1,), (1,)), ((), ())), preferred_element_type=F32)


def _dot_tn(a, b):
    return lax.dot_general(a, b, (((0,), (0,)), ((), ())), preferred_element_type=F32)


def _sigmoid(x):
    return 1.0 / (1.0 + jnp.exp(-x))


def _silu(x):
    return x * _sigmoid(x)


def _gelu_tanh(x):
    return 0.5 * x * (1.0 + jnp.tanh(math.sqrt(2.0 / math.pi) * (x + 0.044715 * (x * x * x))))


def _tile_valid(t, tm):
    return t * tm + lax.broadcasted_iota(jnp.int32, (tm, 1), 0) >= PAD


def _ffn_tail(x, t, g_ref, wup_ref, cw_ref, cb_ref, wdn_ref, o_ref, carry_ref, act_ref):
    tm = x.shape[0]
    hidden = wdn_ref.shape[0]
    valid = _tile_valid(t, tm)
    x = jnp.where(valid, x, 0.0)
    a = _rms(x, g_ref[...]).astype(BF16)
    rid = lax.broadcasted_iota(jnp.int32, (tm, 1), 0)

    @pl.when(t == 0)
    def _():
        carry_ref[...] = jnp.zeros_like(carry_ref)

    def conv_slice(col0):
        cols = slice(col0, col0 + FFN_COLS)
        u = _dot(a, wup_ref[:, cols])
        prev = carry_ref[:, cols]
        p0, p1 = prev[6:7], prev[7:8]
        u1 = jnp.where(rid == 0, p1, pltpu.roll(u, 1, 0))
        u2 = jnp.where(rid == 0, p0, jnp.where(rid == 1, p1, pltpu.roll(u, 2, 0)))
        carry_ref[:, cols] = u[tm - 8:tm]
        w = cw_ref[:, cols]
        return w[0:1] * u2 + w[1:2] * u1 + w[2:3] * u + cb_ref[:, cols]

    for c in range(hidden // FFN_COLS):
        gate = conv_slice(c * FFN_COLS)
        val = conv_slice(hidden + c * FFN_COLS)
        act_ref[:, c * FFN_COLS:(c + 1) * FFN_COLS] = (_silu(gate) * val).astype(BF16)
    y = _dot(act_ref[...], wdn_ref[...])
    o_ref[...] = jnp.where(valid, x + y, 0.0)


def _ffn_proj_kernel(h_ref, o_in_ref, wo_ref, g_ref, wup_ref, cw_ref, cb_ref, wdn_ref,
                     o_ref, carry_ref, act_ref):
    t = pl.program_id(1)
    x = h_ref[...] + _dot(o_in_ref[...], wo_ref[...])
    _ffn_tail(x, t, g_ref, wup_ref, cw_ref, cb_ref, wdn_ref, o_ref, carry_ref, act_ref)


def _ffn_s5_kernel(h_ref, y_ref, gmix_ref, d_ref, wglu_ref, g_ref, wup_ref, cw_ref, cb_ref, wdn_ref,
                   o_ref, carry_ref, act_ref):
    t = pl.program_id(1)
    h = h_ref[...]
    d_model = h.shape[1]
    a = _rms(h, gmix_ref[...])
    z = _gelu_tanh(y_ref[...].astype(F32) + d_ref[...] * a).astype(BF16)
    val = _dot(z, wglu_ref[:, :d_model])
    gate = _dot(z, wglu_ref[:, d_model:])
    x = h + val * _sigmoid(gate)
    _ffn_tail(x, t, g_ref, wup_ref, cw_ref, cb_ref, wdn_ref, o_ref, carry_ref, act_ref)


def _ffn_call(kernel, h, lead_inputs, lead_specs, ffn_w):
    b, lp, d = h.shape
    g, wup, cw, cb, wdn = ffn_w
    hidden = wdn.shape[0]
    tm = FFN_ROW_TILE if lp % FFN_ROW_TILE == 0 else ROW_TILE
    row_spec = lambda width: pl.BlockSpec((None, tm, width), lambda i, t: (i, t, 0))
    return pl.pallas_call(
        kernel,
        name=kernel.__name__.strip('_'),
        grid=(b, lp // tm),
        in_specs=[row_spec(d)] + lead_specs(row_spec) + [
            _resident(g.shape), _resident(wup.shape), _resident(cw.shape), _resident(cb.shape),
            _resident(wdn.shape)],
        out_specs=row_spec(d),
        out_shape=jax.ShapeDtypeStruct((b, lp, d), F32),
        scratch_shapes=[pltpu.VMEM((8, 2 * hidden), F32), pltpu.VMEM((tm, hidden), BF16)],
        compiler_params=_cparams("arbitrary", "arbitrary"),
    )(h, *lead_inputs, g, wup, cw, cb, wdn)


def _ffn_after_proj(h, o_in, wo, ffn_w):
    return _ffn_call(_ffn_proj_kernel, h, (o_in, wo),
                     lambda row_spec: [row_spec(o_in.shape[-1]), _resident(wo.shape)], ffn_w)


def _ffn_after_s5(h, y, gmix, dskip, wglu, ffn_w):
    return _ffn_call(_ffn_s5_kernel, h, (y, gmix, dskip, wglu),
                     lambda row_spec: [row_spec(y.shape[-1]), _resident(gmix.shape), _resident(dskip.shape),
                                       _resident(wglu.shape)], ffn_w)


def _rope_swap(r):
    lane = lax.broadcasted_iota(jnp.int32, r.shape, 1)
    half = MLA_ROPE // 2
    return jnp.where(lane < half, pltpu.roll(r, LANES - half, 1), pltpu.roll(r, half, 1))


def _mla_qkv_kernel(h_ref, gmix_ref, wd_ref, gcq_ref, gckv_ref, wuq_ref, wukv_ref,
                    gqn_ref, gqr_ref, gkn_ref, gkr_ref, qone_ref, kshift_ref, cos_ref, sin_ref,
                    q_ref, k_ref, v_ref):
    a = _rms(h_ref[...], gmix_ref[...]).astype(BF16)
    down = _dot(a, wd_ref[...])
    kv_lo = MLA_Q_LORA + MLA_KV_LORA
    cq = _rms(down[:, :MLA_Q_LORA], gcq_ref[...]).astype(BF16)
    ckv = _rms(down[:, MLA_Q_LORA:kv_lo], gckv_ref[...]).astype(BF16)
    kpe = down[:, kv_lo:kv_lo + LANES]
    q = _dot(cq, wuq_ref[...])
    kv = _dot(ckv, wukv_ref[...])
    cos, sin = cos_ref[...], sin_ref[...]

    def rope(r):
        return r * cos + _rope_swap(r) * sin

    hw = 2 * LANES
    pair_ones = (lax.broadcasted_iota(jnp.int32, (hw, hw), 0) // LANES
                 == lax.broadcasted_iota(jnp.int32, (hw, hw), 1) // LANES).astype(BF16)

    def row_sums(x0, x1):
        both = _dot(jnp.concatenate([x0, x1], axis=1).astype(BF16), pair_ones)
        return both[:, :LANES], both[:, LANES:]

    sq_pe = kpe * kpe
    ss_pe, _ = row_sums(sq_pe, sq_pe)
    k_rope = rope(kpe * gkr_ref[...])
    scale = MLA_QK ** -0.5 * math.log2(math.e)
    for h0 in range(0, MLA_HEADS, 2):
        pair = (h0, h0 + 1)
        qn = [q[:, hd * hw:hd * hw + LANES] for hd in pair]
        qr = [q[:, hd * hw + LANES:(hd + 1) * hw] for hd in pair]
        kn = [kv[:, hd * hw:hd * hw + LANES] for hd in pair]
        ss_q = row_sums(qn[0] * qn[0] + qr[0] * qr[0], qn[1] * qn[1] + qr[1] * qr[1])
        ss_k = row_sums(kn[0] * kn[0], kn[1] * kn[1])
        for n, hd in enumerate(pair):
            inv_q = lax.rsqrt(ss_q[n] / MLA_QK + EPS) * scale
            q_ref[:, hd * hw:hd * hw + LANES] = (qn[n] * inv_q * gqn_ref[...]).astype(BF16)
            q_ref[:, hd * hw + LANES:(hd + 1) * hw] = (
                rope(qr[n] * gqr_ref[...]) * inv_q + qone_ref[...]).astype(BF16)
            inv_k = lax.rsqrt((ss_k[n] + ss_pe) / MLA_QK + EPS)
            k_ref[:, hd * hw:hd * hw + LANES] = (kn[n] * inv_k * gkn_ref[...]).astype(BF16)
            k_ref[:, hd * hw + LANES:(hd + 1) * hw] = (k_rope * inv_k + kshift_ref[...]).astype(BF16)
            v_ref[:, hd * LANES:(hd + 1) * LANES] = kv[:, hd * hw + LANES:(hd + 1) * hw].astype(BF16)


def _attn_kernel(q_ref, k_ref, v_ref, o_ref, ve_s):
    lp, dv = v_ref.shape
    blk, subs = ATT_BLOCK, ATT_SUBS
    n_sup = (lp - FRONT) // (blk * subs)
    ve_s[:, 0:dv] = v_ref[...]
    ve_s[:, dv:2 * dv] = jnp.ones((lp, dv), BF16)
    k0, v0 = k_ref[0:FRONT], ve_s[0:FRONT]
    key_ok0 = lax.broadcasted_iota(jnp.int32, (1, FRONT), 1) >= PAD
    chunk = FRONT // 2
    diag_ok = (lax.broadcasted_iota(jnp.int32, (blk, blk), 0) // chunk
               >= lax.broadcasted_iota(jnp.int32, (blk, blk), 1) // chunk)

    def first_block(q):
        s = jnp.where(key_ok0, _dot_nt(q, k0), NEG_INF)
        m = jnp.max(s, axis=-1, keepdims=True)
        return m, _dot(jnp.exp2(s - m).astype(BF16), v0)

    def update(carry, s, vb):
        m, acc = carry
        m_new = jnp.maximum(m, jnp.max(s, axis=-1, keepdims=True))
        p = jnp.exp2(s - m_new).astype(BF16)
        return m_new, jnp.exp2(m - m_new) * acc + _dot(p, vb)

    def finish(carry):
        acc = carry[1]
        return (acc[:, 0:dv] / acc[:, dv:2 * dv]).astype(o_ref.dtype)

    o_ref[0:FRONT] = finish(first_block(q_ref[0:FRONT]))

    def sup_body(i, _):
        base = FRONT + i * (blk * subs)
        rows = [pl.ds(pl.multiple_of(base + a * blk, LANES), blk) for a in range(subs)]

        def key_rows(j):
            return pl.ds(pl.multiple_of(FRONT + j * blk, LANES), blk)

        def scores(j):
            kb = k_ref[key_rows(j)]
            return tuple(_dot_nt(q_ref[rows[a]], kb) for a in range(subs))

        def kv_body(j, state):
            carries, s_cur = state
            s_next = scores(j + 1)
            vb = ve_s[key_rows(j)]
            return tuple(update(carries[a], s_cur[a], vb) for a in range(subs)), s_next

        carries, s_cur = lax.fori_loop(0, i * subs, kv_body,
                                       (tuple(first_block(q_ref[r]) for r in rows), scores(0)))
        carries = list(carries)
        for c in range(subs):
            if c > 0:
                kb = k_ref[rows[c]]
                s_cur = [None] * c + [_dot_nt(q_ref[rows[a]], kb) for a in range(c, subs)]
            vb = ve_s[rows[c]]
            for a in range(c, subs):
                s = jnp.where(diag_ok, s_cur[a], NEG_INF) if a == c else s_cur[a]
                carries[a] = update(carries[a], s, vb)
        for a in range(subs):
            o_ref[rows[a]] = finish(carries[a])
        return 0

    lax.fori_loop(0, n_sup, sup_body, 0)


def _attn_bounded_kernel(q_ref, k_ref, v_ref, o_ref, ve_s):
    lp, dv = v_ref.shape
    blk, subs = ATT_BLOCK, ATT_SUBS
    n_sup = (lp - FRONT) // (blk * subs)
    ve_s[:, 0:dv] = v_ref[...]
    ve_s[:, dv:2 * dv] = jnp.ones((lp, dv), BF16)
    k0, v0 = k_ref[0:FRONT], ve_s[0:FRONT]
    key_ok0 = lax.broadcasted_iota(jnp.int32, (1, FRONT), 1) >= PAD
    chunk = FRONT // 2
    per_blk = blk // chunk
    lag = (lax.broadcasted_iota(jnp.int32, (blk, 2 * blk), 1) // chunk
           - lax.broadcasted_iota(jnp.int32, (blk, 2 * blk), 0) // chunk)

    def front_part(q):
        return _dot(jnp.exp2(jnp.where(key_ok0, _dot_nt(q, k0), NEG_INF)).astype(BF16), v0)

    def finish(acc):
        return (acc[:, 0:dv] / acc[:, dv:2 * dv]).astype(o_ref.dtype)

    o_ref[0:FRONT] = finish(front_part(q_ref[0:FRONT]))

    def sup_body(i, _):
        base = FRONT + i * (blk * subs)
        rows = [pl.ds(pl.multiple_of(base + a * blk, LANES), blk) for a in range(subs)]

        def key_rows(j):
            return pl.ds(pl.multiple_of(FRONT + j * 2 * blk, LANES), 2 * blk)

        def weights(j, first=0):
            kb = k_ref[key_rows(j)]
            out = []
            for a in range(first, subs):
                limit = (i * subs + a - 2 * j) * per_blk
                out.append(jnp.exp2(jnp.where(lag <= limit, _dot_nt(q_ref[rows[a]], kb), NEG_INF)).astype(BF16))
            return tuple(out)

        def kv_body(j, state):
            accs, p_cur = state
            p_next = weights(j + 1)
            vb = ve_s[key_rows(j)]
            return tuple(acc + _dot(p, vb) for acc, p in zip(accs, p_cur)), p_next

        own = i * (subs // 2)
        accs, p_cur = lax.fori_loop(0, own, kv_body,
                                    (tuple(front_part(q_ref[r]) for r in rows), weights(0)))
        p_last = weights(own + 1, first=subs // 2)
        vb, vb_last = ve_s[key_rows(own)], ve_s[key_rows(own + 1)]
        for a in range(subs):
            acc = accs[a] + _dot(p_cur[a], vb)
            if a >= subs // 2:
                acc = acc + _dot(p_last[a - subs // 2], vb_last)
            o_ref[rows[a]] = finish(acc)
        return 0

    lax.fori_loop(0, n_sup, sup_body, 0)


def _mla_rope_tables(lp):
    half = MLA_ROPE // 2
    pos = jnp.maximum(jnp.arange(lp, dtype=F32) - PAD, 0.0)
    inv_freq = 1.0 / (ROPE_BASE ** (jnp.arange(0, MLA_ROPE, 2, dtype=F32) / MLA_ROPE))
    ang = pos[:, None] * inv_freq[None, :]
    c, s = jnp.cos(ang), jnp.sin(ang)
    z = jnp.zeros((lp, LANES - MLA_ROPE), F32)
    return jnp.concatenate([c, c, z], axis=1), jnp.concatenate([-s, s, z], axis=1)


def _mla_mixer(h, g_mix, w_down, g_cq, g_ckv, w_uq, w_ukv, g_qhead, g_khead):
    b, lp, d = h.shape
    nh, hw = MLA_HEADS, 2 * LANES
    kv_lo = MLA_Q_LORA + MLA_KV_LORA
    wd = jnp.pad(w_down, ((0, 0), (0, kv_lo + LANES - w_down.shape[1]))).astype(BF16)
    wuq = jnp.pad(w_uq.reshape(MLA_Q_LORA, nh, MLA_QK), ((0, 0), (0, 0), (0, hw - MLA_QK)))
    wuq = wuq.reshape(MLA_Q_LORA, nh * hw).astype(BF16)
    wukv = w_ukv.astype(BF16)
    zpad = jnp.zeros((LANES - MLA_ROPE,), F32)
    gqn, gkn = g_qhead[None, :MLA_NOPE], g_khead[None, :MLA_NOPE]
    gqr = jnp.concatenate([g_qhead[MLA_NOPE:], zpad])[None]
    gkr = jnp.concatenate([g_khead[MLA_NOPE:], zpad])[None]
    cos, sin = _mla_rope_tables(lp)
    shift = (math.log2(math.e) * math.sqrt(MLA_QK)) * jnp.max(jnp.abs(g_qhead)) * jnp.max(jnp.abs(g_khead))
    spare = jnp.arange(LANES) == MLA_ROPE
    qone = jnp.where(spare, 1.0, 0.0).astype(F32)[None]
    kshift = jnp.where(spare, -shift, 0.0).astype(F32)[None]
    tm = ROW_TILE
    row_spec = lambda width: pl.BlockSpec((None, tm, width), lambda i, t: (i, t, 0))
    tab_spec = pl.BlockSpec((tm, LANES), lambda i, t: (t, 0))
    small = [g_mix[None], wd, g_cq[None], g_ckv[None], wuq, wukv, gqn, gqr, gkn, gkr, qone, kshift]
    q, k, v = pl.pallas_call(
        _mla_qkv_kernel,
        name='mla_qkv',
        grid=(b, lp // tm),
        in_specs=[row_spec(d)] + [_resident(w.shape) for w in small] + [tab_spec, tab_spec],
        out_specs=[row_spec(nh * hw), row_spec(nh * hw), row_spec(nh * MLA_V)],
        out_shape=[jax.ShapeDtypeStruct((b, lp, nh * hw), BF16), jax.ShapeDtypeStruct((b, lp, nh * hw), BF16),
                   jax.ShapeDtypeStruct((b, lp, nh * MLA_V), BF16)],
        compiler_params=_cparams("parallel", "parallel"),
    )(h, *small, cos, sin)
    head_spec = lambda width: pl.BlockSpec((None, lp, width), lambda i, j: (i, 0, j))

    def attend(body, name):
        return pl.pallas_call(
            body,
            name=name,
            grid=(b, nh),
            in_specs=[head_spec(hw), head_spec(hw), head_spec(MLA_V)],
            out_specs=head_spec(MLA_V),
            out_shape=jax.ShapeDtypeStruct((b, lp, nh * MLA_V), BF16),
            scratch_shapes=[pltpu.VMEM((lp, 2 * MLA_V), BF16)],
            compiler_params=_cparams("parallel", "parallel"),
        )

    return lax.cond(shift <= ATT_MAX_SHIFT,
                    attend(_attn_bounded_kernel, 'mla_attention_bounded'),
                    attend(_attn_kernel, 'mla_attention'), q, k, v)


def _hgrn_kernel(h_ref, gmix_ref, win_ref, lb_ref, go_ref, tri_ref, o_ref,
                 state_ref, q_s, k_s, v_s, lf_s, gate_s):
    t = pl.program_id(1)
    tm, d = h_ref.shape
    dk = HGRN_DK
    c = HGRN_CHUNK

    @pl.when(t == 0)
    def _():
        state_ref[...] = jnp.zeros_like(state_ref)

    a = _rms(h_ref[...], gmix_ref[...]).astype(BF16)
    lb = lb_ref[...]
    q_s[...] = _silu(_dot(a, win_ref[:, 0:d]))
    forget = lb + (1.0 - lb) * _sigmoid(_dot(a, win_ref[:, d:2 * d]))
    lf_s[...] = jnp.log(forget)
    k_s[...] = 1.0 - forget
    v_s[...] = _dot(a, win_ref[:, 2 * d:3 * d]).astype(BF16)
    gate_s[...] = _silu(_dot(a, win_ref[:, 3 * d:4 * d]))
    causal = (lax.broadcasted_iota(jnp.int32, (c, c), 0) >= lax.broadcasted_iota(jnp.int32, (c, c), 1))
    tri = tri_ref[...]

    def chunk_body(ci, _):
        rows = pl.ds(pl.multiple_of(ci * c, c), c)
        lf = lf_s[rows]
        lf_hi = lf.astype(BF16)
        lf_lo = (lf - lf_hi.astype(F32)).astype(BF16)
        cum = _dot(tri, lf_hi) + _dot(tri, lf_lo)
        ref_row = cum[HGRN_REF:HGRN_REF + 1]
        last = cum[c - 1:c]
        q, k, v = q_s[rows], k_s[rows], v_s[rows]
        q_rel = (q * jnp.exp(cum - ref_row)).astype(BF16)
        k_rel = (k * jnp.exp(ref_row - cum)).astype(BF16)
        q_dec = (q * jnp.exp(cum)).astype(BF16)
        k_tail = (k * jnp.exp(last - cum)).astype(BF16)
        decay = jnp.exp(last)
        gate = gate_s[rows]
        heads = [slice(hd * dk, (hd + 1) * dk) for hd in range(d // dk)]
        attn = [jnp.where(causal, _dot_nt(q_rel[:, sl], k_rel[:, sl]), 0.0).astype(BF16) for sl in heads]
        inter = [_dot_nt(q_dec[:, sl], state_ref[hd].astype(BF16)) for hd, sl in enumerate(heads)]
        for hd, sl in enumerate(heads):
            state_ref[hd] = state_ref[hd] * decay[:, sl] + _dot_tn(v[:, sl], k_tail[:, sl])
        for hd, sl in enumerate(heads):
            o = _dot(attn[hd], v[:, sl]) + inter[hd]
            o_ref[rows, sl] = (_rms(o, go_ref[...]) * gate[:, sl]).astype(o_ref.dtype)
        return 0

    lax.fori_loop(0, tm // c, chunk_body, 0, unroll=True)


def _hgrn_mixer(h, g_mix, w_in, lb, g_o):
    b, lp, d = h.shape
    tm = ROW_TILE
    c = HGRN_CHUNK
    tri = (jnp.arange(c)[:, None] >= jnp.arange(c)[None, :]).astype(BF16)
    row_spec = pl.BlockSpec((None, tm, d), lambda i, t: (i, t, 0))
    small = [g_mix[None], w_in.astype(BF16), lb[None], g_o[None], tri]
    return pl.pallas_call(
        _hgrn_kernel,
        name='hgrn2',
        grid=(b, lp // tm),
        in_specs=[row_spec] + [_resident(w.shape) for w in small],
        out_specs=row_spec,
        out_shape=jax.ShapeDtypeStruct((b, lp, d), BF16),
        scratch_shapes=[pltpu.VMEM((d // HGRN_DK, d // HGRN_HEADS, HGRN_DK), F32),
                        pltpu.VMEM((tm, d), F32), pltpu.VMEM((tm, d), F32), pltpu.VMEM((tm, d), BF16),
                        pltpu.VMEM((tm, d), F32), pltpu.VMEM((tm, d), F32)],
        compiler_params=_cparams("arbitrary", "arbitrary"),
    )(h, *small)


def _norm_kernel(h_ref, g_ref, o_ref):
    o_ref[...] = _rms(h_ref[...], g_ref[...]).astype(o_ref.dtype)


def _s5_kernel(a_ref, perm_ref, mw_ref, toe_ref, xm_ref, a1_ref, a2_ref, y_ref, w_s, yt_s):
    nbs, lp, _ = a_ref.shape
    t = S5_BLOCK
    nblk = lp // t
    gc = xm_ref.shape[0]
    width = 4 * S5_STATE
    half = width // 2
    steps = jnp.concatenate(
        [jnp.concatenate([a_ref[bb, pl.ds(s, nblk, stride=t), :] for s in range(t)], axis=1) for bb in range(nbs)],
        axis=0).astype(BF16)
    u_all = _dot(steps, perm_ref[...]).astype(BF16)
    for g in range(gc):
        u = u_all[:, g * width:(g + 1) * width]
        w_s[:, g * width:(g + 1) * width] = _dot(u, mw_ref[g])
        yt_s[:, g * width:(g + 1) * width] = _dot(u, toe_ref[g])
    a1, a2 = a1_ref[...], a2_ref[...]

    def swap_halves(z):
        parts = []
        for g in range(gc):
            parts += [z[:, g * width + half:(g + 1) * width], z[:, g * width:g * width + half]]
        return jnp.concatenate(parts, axis=1)

    def step(blk, zs):
        nxt = []
        for bb, z in enumerate(zs):
            row = pl.ds(bb * nblk + blk, 1)
            w = w_s[row, :]
            w_s[row, :] = z
            nxt.append(a1 * z + a2 * swap_halves(z) + w)
        return tuple(nxt)

    lax.fori_loop(0, nblk, step, tuple(jnp.zeros((1, gc * width), F32) for _ in range(nbs)))
    ys = []
    for g in range(gc):
        x_start = w_s[:, g * width:g * width + half].astype(BF16)
        ys.append((yt_s[:, g * width:(g + 1) * width] + _dot(x_start, xm_ref[g])).astype(BF16))
    out = _dot_nt(jnp.concatenate(ys, axis=1), perm_ref[...])
    for bb in range(nbs):
        for s in range(t):
            y_ref[bb, pl.ds(s, nblk, stride=t), :] = out[bb * nblk:(bb + 1) * nblk, s * LANES:(s + 1) * LANES]


def _s5_operators(lam_re, lam_im, log_dt, b_re, b_im, c_re, c_im):
    hp = lax.Precision.HIGHEST
    t = S5_BLOCK
    dt = jnp.exp(log_dt)[:, None]
    mag = jnp.exp(lam_re * dt)
    abar_re, abar_im = mag * jnp.cos(lam_im * dt), mag * jnp.sin(lam_im * dt)
    den = lam_re * lam_re + lam_im * lam_im
    zoh_re = ((abar_re - 1.0) * lam_re + abar_im * lam_im) / den
    zoh_im = (abar_im * lam_re - (abar_re - 1.0) * lam_im) / den
    bb_re = zoh_re[..., None] * b_re - zoh_im[..., None] * b_im
    bb_im = zoh_re[..., None] * b_im + zoh_im[..., None] * b_re
    steps = jnp.arange(t + 1, dtype=F32)[:, None, None] * dt[None]
    pmag = jnp.exp(lam_re[None] * steps)
    pw_re, pw_im = pmag * jnp.cos(lam_im[None] * steps), pmag * jnp.sin(lam_im[None] * steps)
    cp_re = c_re[None] * pw_re[:t, :, None, :] - c_im[None] * pw_im[:t, :, None, :]
    cp_im = c_re[None] * pw_im[:t, :, None, :] + c_im[None] * pw_re[:t, :, None, :]
    taps = jnp.einsum("tgjp,gpk->gktj", jnp.concatenate([cp_re, -cp_im], axis=-1),
                      jnp.concatenate([bb_re, bb_im], axis=1), precision=hp)
    ar_t = jnp.arange(t)
    place = (ar_t[None, :, None] - ar_t[:, None, None] == ar_t[None, None, :]).astype(BF16)
    toe = jnp.einsum("stu,gkuj->gsktj", place, taps.astype(BF16)).reshape(-1, t * S5_GROUP, t * S5_GROUP)
    rev_re, rev_im = pw_re[t - 1::-1][:t], pw_im[t - 1::-1][:t]
    w_re = rev_re[..., None] * bb_re[None] - rev_im[..., None] * bb_im[None]
    w_im = rev_re[..., None] * bb_im[None] + rev_im[..., None] * bb_re[None]
    to_rows = lambda m: m.transpose(1, 0, 3, 2).reshape(m.shape[1], t * S5_GROUP, S5_STATE)
    w_re, w_im = to_rows(w_re), to_rows(w_im)
    mw = jnp.concatenate([w_re, w_im, w_im, w_re], axis=-1)
    cq_re = c_re[None] * pw_re[1:, :, None, :] - c_im[None] * pw_im[1:, :, None, :]
    cq_im = c_re[None] * pw_im[1:, :, None, :] + c_im[None] * pw_re[1:, :, None, :]
    to_cols = lambda m: m.transpose(1, 3, 0, 2).reshape(m.shape[1], S5_STATE, t * S5_GROUP)
    xm = jnp.concatenate([to_cols(cq_re), -to_cols(cq_im)], axis=1)
    ar, ai = pw_re[t], pw_im[t]
    a1 = jnp.concatenate([ar, ar, ar, ar], axis=-1)
    a2 = jnp.concatenate([-ai, ai, ai, -ai], axis=-1)
    per_slab = lambda m: m.reshape(-1, 1, S5_SLAB_GROUPS * 4 * S5_STATE)
    return mw.astype(BF16), toe, xm.astype(BF16), per_slab(a1), per_slab(a2)


def _s5_permutation():
    t, kk = S5_BLOCK, S5_GROUP
    row = jnp.arange(t * LANES)
    s, lane = row // LANES, row % LANES
    target = (lane // kk) * (t * kk) + s * kk + lane % kk
    return (target[:, None] == row[None, :]).astype(BF16)


def _s5_mixer(h, g_mix, lam_re, lam_im, log_dt, b_re, b_im, c_re, c_im):
    b, lp, d = h.shape
    t, kk = S5_BLOCK, S5_GROUP
    nblk = lp // t
    tm = ROW_TILE
    row_spec = pl.BlockSpec((None, tm, d), lambda i, j: (i, j, 0))
    a = pl.pallas_call(
        _norm_kernel,
        name='s5_norm',
        grid=(b, lp // tm),
        in_specs=[row_spec, _resident((1, d))],
        out_specs=row_spec,
        out_shape=jax.ShapeDtypeStruct((b, lp, d), F32),
        compiler_params=_cparams("parallel", "parallel"),
    )(h, g_mix[None])
    mw, toe, xm, a1, a2 = _s5_operators(lam_re, lam_im, log_dt, b_re, b_im, c_re, c_im)
    sel = _s5_permutation()
    gc = S5_SLAB_GROUPS
    width = 4 * S5_STATE
    nbs = S5_BATCH_ROWS if b % S5_BATCH_ROWS == 0 else 1
    slab_spec = pl.BlockSpec((nbs, lp, LANES), lambda i, j: (j, 0, i))
    grp_spec = lambda r, c: pl.BlockSpec((gc, r, c), lambda i, j: (i, 0, 0))
    coef_spec = pl.BlockSpec((None, 1, gc * width), lambda i, j: (i, 0, 0))
    return pl.pallas_call(
        _s5_kernel,
        name='s5_scan',
        grid=(d // LANES, b // nbs),
        in_specs=[slab_spec, _resident(sel.shape), grp_spec(t * kk, width), grp_spec(t * kk, t * kk),
                  grp_spec(2 * S5_STATE, t * kk), coef_spec, coef_spec],
        out_specs=slab_spec,
        out_shape=jax.ShapeDtypeStruct((b, lp, d), F32),
        scratch_shapes=[pltpu.VMEM((nbs * nblk, gc * width), F32), pltpu.VMEM((nbs * nblk, gc * width), F32)],
        compiler_params=_cparams("parallel", "parallel"),
    )(a, sel, mw, toe, xm, a1, a2)


def _ret_kernel(h_ref, gmix_ref, win_ref, cos_ref, sin_ref, gn_ref, o_ref, state_ref, q_s, k_s, v_s, g_s):
    t = pl.program_id(1)
    tm, d = h_ref.shape
    nh = RET_HEADS
    dk = d // nh
    dv = 2 * dk
    c = RET_CHUNK
    half = dk // 2

    @pl.when(t == 0)
    def _():
        state_ref[...] = jnp.zeros_like(state_ref)

    a = _rms(h_ref[...], gmix_ref[...]).astype(BF16)
    cos, sin = cos_ref[...], sin_ref[...]
    for hd in range(nh):
        for ref, col0, scl in ((q_s, 0, 1.0), (k_s, d, dk ** -0.5)):
            x = _dot(a, win_ref[:, col0 + hd * dk:col0 + (hd + 1) * dk])
            x1, x2 = x[:, :half], x[:, half:]
            ref[:, hd * dk:hd * dk + half] = (x1 * cos - x2 * sin) * scl
            ref[:, hd * dk + half:(hd + 1) * dk] = (x1 * sin + x2 * cos) * scl
        v_s[:, hd * dv:(hd + 1) * dv] = _dot(a, win_ref[:, 2 * d + hd * dv:2 * d + (hd + 1) * dv]).astype(BF16)
        g_s[:, hd * dv:(hd + 1) * dv] = _silu(_dot(a, win_ref[:, 4 * d + hd * dv:4 * d + (hd + 1) * dv]))
    ri = lax.broadcasted_iota(jnp.int32, (c, c), 0)
    ci_ = lax.broadcasted_iota(jnp.int32, (c, c), 1)
    diff = (ri - ci_).astype(F32)
    pos = lax.broadcasted_iota(jnp.int32, (c, 1), 0).astype(F32)

    def chunk_body(cc, _):
        rows = pl.ds(pl.multiple_of(cc * c, c), c)
        log_g = [math.log(1.0 - 2.0 ** (-5.0 - hd)) for hd in range(nh)]
        scores, inter = [], []
        for hd, lg in enumerate(log_g):
            q = q_s[rows, hd * dk:(hd + 1) * dk]
            k = k_s[rows, hd * dk:(hd + 1) * dk]
            v = v_s[rows, hd * dv:(hd + 1) * dv]
            st = state_ref[hd]
            decay = jnp.where(diff >= 0, jnp.exp(diff * lg), 0.0)
            scores.append((_dot_nt(q.astype(BF16), k.astype(BF16)) * decay).astype(BF16))
            q_in = (q * jnp.exp((pos + 1.0) * lg)).astype(BF16)
            k_out = (k * jnp.exp((c - 1.0 - pos) * lg)).astype(BF16)
            inter.append(_dot(q_in, st.astype(BF16)))
            state_ref[hd] = st * math.exp(c * lg) + _dot_tn(k_out, v)
        for hd in range(nh):
            o = _dot(scores[hd], v_s[rows, hd * dv:(hd + 1) * dv]) + inter[hd]
            mu = jnp.mean(o, axis=-1, keepdims=True)
            oc = o - mu
            var = jnp.mean(oc * oc, axis=-1, keepdims=True)
            o = oc * lax.rsqrt(var + EPS) * gn_ref[:, hd * dv:(hd + 1) * dv] * g_s[rows, hd * dv:(hd + 1) * dv]
            o_ref[rows, hd * dv:(hd + 1) * dv] = o.astype(o_ref.dtype)
        return 0

    lax.fori_loop(0, tm // c, chunk_body, 0, unroll=True)


def _ret_mixer(h, g_mix, w_in, gn_g):
    b, lp, d = h.shape
    nh = RET_HEADS
    dk = d // nh
    tm = ROW_TILE
    pos = jnp.maximum(jnp.arange(lp, dtype=F32) - PAD, 0.0)
    inv_freq = 1.0 / (ROPE_BASE ** (jnp.arange(0, dk, 2, dtype=F32) / dk))
    ang = pos[:, None] * inv_freq[None, :]
    cos, sin = jnp.cos(ang), jnp.sin(ang)
    row_spec = lambda width: pl.BlockSpec((None, tm, width), lambda i, t: (i, t, 0))
    tab_spec = pl.BlockSpec((tm, dk // 2), lambda i, t: (t, 0))
    win = w_in.astype(BF16)
    return pl.pallas_call(
        _ret_kernel,
        name='retention',
        grid=(b, lp // tm),
        in_specs=[row_spec(d), _resident((1, d)), _resident(win.shape), tab_spec, tab_spec,
                  _resident((1, 2 * d))],
        out_specs=row_spec(2 * d),
        out_shape=jax.ShapeDtypeStruct((b, lp, 2 * d), BF16),
        scratch_shapes=[pltpu.VMEM((nh, dk, 2 * dk), F32), pltpu.VMEM((tm, d), F32), pltpu.VMEM((tm, d), F32),
                        pltpu.VMEM((tm, 2 * d), BF16), pltpu.VMEM((tm, 2 * d), F32)],
        compiler_params=_cparams("arbitrary", "arbitrary"),
    )(h, g_mix[None], win, cos, sin, gn_g[None])


def kernel(x, meta_tokens, norm_mix_g, norm_ffn_g, mla_w_down, mla_cq_norm_g, mla_ckv_norm_g, mla_w_uq, mla_w_ukv, mla_q_head_g, mla_k_head_g, mla_w_o, hgrn_w_in, hgrn_lb_logits, hgrn_o_norm_g, hgrn_w_o, s5_lam_re, s5_lam_im, s5_log_dt, s5_b_re, s5_b_im, s5_c_re, s5_c_im, s5_d, s5_w_glu, ret_w_in, ret_gn_g, ret_w_o, ffn_w_up, ffn_conv_w, ffn_conv_b, ffn_w_down):
    b, seq, d = x.shape
    depth = norm_mix_g.shape[0]
    h = jnp.concatenate([jnp.zeros((b, PAD, d), x.dtype),
                         jnp.broadcast_to(meta_tokens[None].astype(x.dtype), (b, N_META, d)), x], axis=1)
    lb_cum = jnp.cumsum(jax.nn.softmax(hgrn_lb_logits.astype(F32), axis=0), axis=0)
    lb_all = lb_cum - lb_cum[0:1]
    for i in range(depth):
        m, j = i % 4, i // 4
        ffn_w = (norm_ffn_g[i][None], ffn_w_up[i].astype(BF16), ffn_conv_w[i], ffn_conv_b[i][None],
                 ffn_w_down[i].astype(BF16))
        g_mix = norm_mix_g[i]
        if m == 0:
            o = _mla_mixer(h, g_mix, mla_w_down[j], mla_cq_norm_g[j], mla_ckv_norm_g[j], mla_w_uq[j],
                           mla_w_ukv[j], mla_q_head_g[j], mla_k_head_g[j])
            h = _ffn_after_proj(h, o, mla_w_o[j].astype(BF16), ffn_w)
        elif m == 1:
            o = _hgrn_mixer(h, g_mix, hgrn_w_in[j], lb_all[i], hgrn_o_norm_g[j])
            h = _ffn_after_proj(h, o, hgrn_w_o[j].astype(BF16), ffn_w)
        elif m == 2:
            y = _s5_mixer(h, g_mix, s5_lam_re[j], s5_lam_im[j], s5_log_dt[j], s5_b_re[j], s5_b_im[j],
                          s5_c_re[j], s5_c_im[j])
            h = _ffn_after_s5(h, y, g_mix[None], s5_d[j][None], s5_w_glu[j].astype(BF16), ffn_w)
        else:
            o = _ret_mixer(h, g_mix, ret_w_in[j], ret_gn_g[j])
            h = _ffn_after_proj(h, o, ret_w_o[j].astype(BF16), ffn_w)
    return h[:, FRONT:]
```
